```python
import math
import jax, jax.numpy as jnp
from jax import lax
import numpy as np

D_MODEL = 1024
BATCH = 8
SEQ = 2048
DEPTH = 4
DEC_BATCH = 128
DEC_SEQ = 1
PAST_LEN = 16384
PAGE_SIZE = 128

N_META = 16
D_CONV = D_MODEL
CONV_A_WIDTH = 3
CONV_A_GROUPS = 16
D_RNN = D_MODEL
RG_HEADS = 16
RG_HEAD_DIM = D_RNN // RG_HEADS
CONV_B_WIDTH = 4
RG_C = 8.0
D_FF = 2816
FFN_CONV_WIDTH = 3
EPS = 1e-6
D_IN = 3 * D_CONV + 2 * D_RNN + 2 * D_MODEL

kernel_name = "hybrid_shortconv_rglru_convffn_meta_step"


def _rmsnorm(x, g):
    xf = x.astype(jnp.float32)
    y = xf * lax.rsqrt(jnp.mean(xf * xf, axis=-1, keepdims=True) + EPS)
    return (y * g.astype(jnp.float32)).astype(x.dtype)


def _causal_dwconv(x, buf, w):
    width = w.shape[0]
    t = x.shape[1]
    xp = jnp.concatenate([buf.astype(x.dtype), x], axis=1)
    y = xp[:, 0:t] * w[0]
    for k in range(1, width):
        y = y + xp[:, k:k + t] * w[k]
    return y, xp[:, xp.shape[1] - (width - 1):]


def _rg_lru(xc, h0, w_a, b_a, w_x, b_x, lam, is_first):
    bn, t, c = xc.shape
    xh = xc.reshape(bn, t, RG_HEADS, RG_HEAD_DIM)
    r = jax.nn.sigmoid(jnp.einsum('bthi,hij->bthj', xh, w_a).reshape(bn, t, c) + b_a)
    i = jax.nn.sigmoid(jnp.einsum('bthi,hij->bthj', xh, w_x).reshape(bn, t, c) + b_x)
    log_a = -RG_C * r.astype(jnp.float32) * jax.nn.softplus(-lam.astype(jnp.float32))
    a = jnp.exp(log_a)
    mult = jnp.sqrt(-jnp.expm1(2.0 * log_a))
    mult = jnp.where(is_first[None, :, None], 1.0, mult)
    b = mult * (i * xc).astype(jnp.float32)

    def step(h, ab):
        h = ab[0] * h + ab[1]
        return h, h

    h_last, hs = lax.scan(step, h0.astype(jnp.float32),
                          (a.transpose(1, 0, 2), b.transpose(1, 0, 2)))
    return hs.transpose(1, 0, 2).astype(xc.dtype), h_last


def _layer(x, buf_a, buf_b, h0, buf_f, is_first, norm_mix, norm_ffn, w_in, b_gate,
           conv_a_w, w_a_out, conv_b_w, conv_b_b, rg_w_a, rg_b_a, rg_w_x, rg_b_x,
           rg_lambda, w_b_out, w_o, ffn_w_up, ffn_w_gate, ffn_conv_w, ffn_conv_b, ffn_w_down):
    hn = _rmsnorm(x, norm_mix)
    proj = hn @ w_in
    o = 0
    gB = proj[..., o:o + D_CONV]; o += D_CONV
    gC = proj[..., o:o + D_CONV]; o += D_CONV
    ha = proj[..., o:o + D_CONV]; o += D_CONV
    xr = proj[..., o:o + D_RNN]; o += D_RNN
    gr = proj[..., o:o + D_RNN]; o += D_RNN
    merge = proj[..., o:o + 2 * D_MODEL] + b_gate
    conv_a, nbuf_a = _causal_dwconv(gC * ha, buf_a, conv_a_w)
    y_a = (gB * conv_a) @ w_a_out
    xc, nbuf_b = _causal_dwconv(xr, buf_b, conv_b_w)
    xc = xc + conv_b_b
    hs, h_last = _rg_lru(xc, h0, rg_w_a, rg_b_a, rg_w_x, rg_b_x, rg_lambda, is_first)
    y_b = (jax.nn.gelu(gr, approximate=True) * hs) @ w_b_out
    mixed = (jax.nn.sigmoid(merge[..., :D_MODEL]) * y_a
             + jax.nn.sigmoid(merge[..., D_MODEL:]) * y_b)
    x = x + mixed @ w_o
    hf = _rmsnorm(x, norm_ffn)
    u = hf @ ffn_w_up
    uc, nbuf_f = _causal_dwconv(u, buf_f, ffn_conv_w)
    uc = uc + ffn_conv_b
    x = x + (jax.nn.silu(uc) * (hf @ ffn_w_gate)) @ ffn_w_down
    return x, nbuf_a, nbuf_b, h_last.astype(x.dtype), nbuf_f


def _trunk(x, start_pos, bufs_a, bufs_b, hs0, bufs_f, norm_mix, norm_ffn, norm_final, w_in,
           b_gate, conv_a_w, w_a_out, conv_b_w, conv_b_b, rg_w_a, rg_b_a, rg_w_x, rg_b_x,
           rg_lambda, w_b_out, w_o, ffn_w_up, ffn_w_gate, ffn_conv_w, ffn_conv_b, ffn_w_down):
    t = x.shape[1]
    is_first = (start_pos + jnp.arange(t)) == 0
    na, nb, nh, nf = [], [], [], []
    for l in range(DEPTH):
        x, a_, b_, h_, f_ = _layer(
            x, bufs_a[l], bufs_b[l], hs0[l], bufs_f[l], is_first, norm_mix[l], norm_ffn[l],
            w_in[l], b_gate[l], conv_a_w[l], w_a_out[l], conv_b_w[l], conv_b_b[l], rg_w_a[l],
            rg_b_a[l], rg_w_x[l], rg_b_x[l], rg_lambda[l], w_b_out[l], w_o[l], ffn_w_up[l],
            ffn_w_gate[l], ffn_conv_w[l], ffn_conv_b[l], ffn_w_down[l])
        na.append(a_); nb.append(b_); nh.append(h_); nf.append(f_)
    y = _rmsnorm(x, norm_final)
    return y, jnp.stack(na), jnp.stack(nb), jnp.stack(nh), jnp.stack(nf)


def setup_inputs(seed: int = 0) -> dict:
    key = jax.random.key(seed)
    ks = jax.random.split(key, 40)
    f32 = jnp.float32
    nrm = lambda k, shape, s: jax.random.normal(k, shape, f32) * s
    u = jax.random.uniform(ks[30], (DEPTH, D_RNN), f32, 0.9, 0.999)
    s = u ** (1.0 / RG_C)
    rg_lambda = jnp.log(s) - jnp.log1p(-s)
    return {
        "x_prompt": nrm(ks[0], (BATCH, SEQ, D_MODEL), 1.0),
        "x_sample": nrm(ks[1], (DEC_BATCH, DEC_SEQ, D_MODEL), 1.0),
        "state_conv_a": nrm(ks[2], (DEPTH, DEC_BATCH, CONV_A_WIDTH - 1, D_CONV), 1.0),
        "state_conv_b": nrm(ks[3], (DEPTH, DEC_BATCH, CONV_B_WIDTH - 1, D_RNN), 1.0),
        "state_rglru": nrm(ks[4], (DEPTH, DEC_BATCH, D_RNN), 0.5),
        "state_conv_ffn": nrm(ks[5], (DEPTH, DEC_BATCH, FFN_CONV_WIDTH - 1, D_FF), 1.0),
        "meta_tokens": nrm(ks[6], (N_META, D_MODEL), 1.0),
        "norm_mix": 1.0 + nrm(ks[7], (DEPTH, D_MODEL), 0.02),
        "norm_ffn": 1.0 + nrm(ks[8], (DEPTH, D_MODEL), 0.02),
        "norm_final": 1.0 + nrm(ks[9], (D_MODEL,), 0.02),
        "w_in": nrm(ks[10], (DEPTH, D_MODEL, D_IN), D_MODEL ** -0.5),
        "b_gate": nrm(ks[11], (DEPTH, 2 * D_MODEL), 0.02),
        "conv_a_w": nrm(ks[12], (DEPTH, CONV_A_WIDTH, D_CONV), CONV_A_WIDTH ** -0.5),
        "w_a_out": nrm(ks[13], (DEPTH, D_CONV, D_MODEL), D_CONV ** -0.5),
        "conv_b_w": nrm(ks[14], (DEPTH, CONV_B_WIDTH, D_RNN), CONV_B_WIDTH ** -0.5),
        "conv_b_b": nrm(ks[15], (DEPTH, D_RNN), 0.02),
        "rg_w_a": nrm(ks[16], (DEPTH, RG_HEADS, RG_HEAD_DIM, RG_HEAD_DIM), RG_HEAD_DIM ** -0.5),
        "rg_b_a": nrm(ks[17], (DEPTH, D_RNN), 0.02),
        "rg_w_x": nrm(ks[18], (DEPTH, RG_HEADS, RG_HEAD_DIM, RG_HEAD_DIM), RG_HEAD_DIM ** -0.5),
        "rg_b_x": nrm(ks[19], (DEPTH, D_RNN), 0.02),
        "rg_lambda": rg_lambda,
        "w_b_out": nrm(ks[20], (DEPTH, D_RNN, D_MODEL), D_RNN ** -0.5),
        "w_o": nrm(ks[21], (DEPTH, D_MODEL, D_MODEL), D_MODEL ** -0.5),
        "ffn_w_up": nrm(ks[22], (DEPTH, D_MODEL, D_FF), D_MODEL ** -0.5),
        "ffn_w_gate": nrm(ks[23], (DEPTH, D_MODEL, D_FF), D_MODEL ** -0.5),
        "ffn_conv_w": nrm(ks[24], (DEPTH, FFN_CONV_WIDTH, D_FF), FFN_CONV_WIDTH ** -0.5),
        "ffn_conv_b": nrm(ks[25], (DEPTH, D_FF), 0.02),
        "ffn_w_down": nrm(ks[26], (DEPTH, D_FF, D_MODEL), D_FF ** -0.5),
    }


def reference(x_prompt, x_sample, state_conv_a, state_conv_b, state_rglru, state_conv_ffn,
              meta_tokens, norm_mix, norm_ffn, norm_final, w_in, b_gate, conv_a_w, w_a_out,
              conv_b_w, conv_b_b, rg_w_a, rg_b_a, rg_w_x, rg_b_x, rg_lambda, w_b_out, w_o,
              ffn_w_up, ffn_w_gate, ffn_conv_w, ffn_conv_b, ffn_w_down):
    weights = (norm_mix, norm_ffn, norm_final, w_in, b_gate, conv_a_w, w_a_out, conv_b_w,
               conv_b_b, rg_w_a, rg_b_a, rg_w_x, rg_b_x, rg_lambda, w_b_out, w_o, ffn_w_up,
               ffn_w_gate, ffn_conv_w, ffn_conv_b, ffn_w_down)
    bp = x_prompt.shape[0]
    dt = x_prompt.dtype
    xp = jnp.concatenate(
        [jnp.broadcast_to(meta_tokens.astype(dt)[None], (bp, N_META, D_MODEL)), x_prompt], axis=1)
    z_a = jnp.zeros((DEPTH, bp, CONV_A_WIDTH - 1, D_CONV), dt)
    z_b = jnp.zeros((DEPTH, bp, CONV_B_WIDTH - 1, D_RNN), dt)
    z_h = jnp.zeros((DEPTH, bp, D_RNN), dt)
    z_f = jnp.zeros((DEPTH, bp, FFN_CONV_WIDTH - 1, D_FF), dt)
    yp, p_conv_a, p_conv_b, p_rglru, p_conv_ffn = _trunk(xp, 0, z_a, z_b, z_h, z_f, *weights)
    y_prompt = yp[:, N_META:]
    y_sample, s_conv_a, s_conv_b, s_rglru, s_conv_ffn = _trunk(
        x_sample, PAST_LEN, state_conv_a, state_conv_b, state_rglru, state_conv_ffn, *weights)
    return (y_prompt, y_sample, p_conv_a, p_conv_b, p_rglru, p_conv_ffn,
            s_conv_a, s_conv_b, s_rglru, s_conv_ffn)
```

```python
import functools

import jax
import jax.numpy as jnp
from jax import lax
from jax.experimental import pallas as pl
from jax.experimental.pallas import tpu as pltpu

D_MODEL = 1024
DEPTH = 4
N_META = 16
D_CONV = D_MODEL
D_RNN = D_MODEL
RG_HEADS = 16
RG_HEAD_DIM = D_RNN // RG_HEADS
RG_C = 8.0
D_FF = 2816
EPS = 1e-6
PAST_LEN = 16384
CONV_A_TAPS = 3
CONV_B_TAPS = 4
CONV_F_TAPS = 3

MXU_EDGE_V7X = 256
GATE_BLOCKS = D_RNN // MXU_EDGE_V7X
VMEM_LIMIT_BYTES_V7X = 58 * 1024 * 1024

F32 = jnp.float32
BF16 = jnp.bfloat16


def _dot(a, b):
    return jnp.dot(a, b, preferred_element_type=F32)


def _rmsnorm(x, g):
    ms = jnp.mean(x * x, axis=-1, keepdims=True)
    return x * lax.rsqrt(ms + EPS) * g


def _softplus(z):
    return jnp.maximum(z, 0.0) + jnp.log1p(jnp.exp(-jnp.abs(z)))


def _mixer_kernel(x_ref, sa_ref, sb_ref, sh_ref, nm_ref, w_in_ref, bg_ref, caw_ref,
                  w_a_out_ref, cbw_ref, cbb_ref, wg_ref, bga_ref, bgx_ref, lam_ref,
                  w_b_out_ref, w_o_ref,
                  out_ref, oa_ref, ob_ref, oh_ref,
                  exta, extb, a_s, b_s, h_s, *, B, tT, n_tiles, first_pos_is_zero):
    R = B * tT
    i = pl.program_id(0)
    na = (CONV_A_TAPS - 1) * B
    nb = (CONV_B_TAPS - 1) * B

    @pl.when(i == 0)
    def _():
        exta[0:na, :] = sa_ref[...]
        extb[0:nb, :] = sb_ref[...]
        h_s[...] = sh_ref[...]

    x = x_ref[...]
    hn = _rmsnorm(x, nm_ref[...]).astype(BF16)

    g_c = _dot(hn, w_in_ref[:, 1 * D_CONV:2 * D_CONV])
    h_a = _dot(hn, w_in_ref[:, 2 * D_CONV:3 * D_CONV])
    exta[na:na + R, :] = g_c * h_a
    conv_a = caw_ref[0:1, :] * exta[0:R, :]
    for k in range(1, CONV_A_TAPS):
        conv_a = conv_a + caw_ref[k:k + 1, :] * exta[k * B:k * B + R, :]
    g_b = _dot(hn, w_in_ref[:, 0:D_CONV])
    y_a = _dot((g_b * conv_a).astype(BF16), w_a_out_ref[...])
    tail_a = exta[R:R + na, :]
    exta[0:na, :] = tail_a

    o = 3 * D_CONV
    extb[nb:nb + R, :] = _dot(hn, w_in_ref[:, o:o + D_RNN])
    xc = cbw_ref[0:1, :] * extb[0:R, :]
    for k in range(1, CONV_B_TAPS):
        xc = xc + cbw_ref[k:k + 1, :] * extb[k * B:k * B + R, :]
    xc = xc + cbb_ref[...]
    tail_b = extb[R:R + nb, :]
    extb[0:nb, :] = tail_b

    xcb = xc.astype(BF16)
    neg_c_sp = -RG_C * _softplus(-lam_ref[...])
    if first_pos_is_zero:
        rows = lax.broadcasted_iota(jnp.int32, (R, MXU_EDGE_V7X), 0)
        is_first = jnp.logical_and(i == 0, rows < B)
    for j in range(GATE_BLOCKS):
        c0, c1 = j * MXU_EDGE_V7X, (j + 1) * MXU_EDGE_V7X
        g = _dot(xcb[:, c0:c1], wg_ref[j])
        r_g = jax.nn.sigmoid(g[:, :MXU_EDGE_V7X] + bga_ref[:, c0:c1])
        i_g = jax.nn.sigmoid(g[:, MXU_EDGE_V7X:] + bgx_ref[:, c0:c1])
        log_a = neg_c_sp[:, c0:c1] * r_g
        a = jnp.exp(log_a)
        s = jnp.tanh(-log_a)
        mult = jnp.sqrt(2.0 * s / (1.0 + s))
        if first_pos_is_zero:
            mult = jnp.where(is_first, 1.0, mult)
        a_s[:, c0:c1] = a
        b_s[:, c0:c1] = mult * (i_g * xc[:, c0:c1])

    def step(t, h):
        r0 = pl.multiple_of(t * B, B)
        h = a_s[pl.ds(r0, B), :] * h + b_s[pl.ds(r0, B), :]
        b_s[pl.ds(r0, B), :] = h
        return h

    h_last = lax.fori_loop(0, tT, step, h_s[...], unroll=True)
    h_s[...] = h_last

    o = 3 * D_CONV + D_RNN
    g_r = _dot(hn, w_in_ref[:, o:o + D_RNN])
    y_b = _dot((jax.nn.gelu(g_r, approximate=True) * b_s[...]).astype(BF16), w_b_out_ref[...])

    o = 3 * D_CONV + 2 * D_RNN
    m_a = _dot(hn, w_in_ref[:, o:o + D_MODEL]) + bg_ref[:, 0:D_MODEL]
    m_b = _dot(hn, w_in_ref[:, o + D_MODEL:o + 2 * D_MODEL]) + bg_ref[:, D_MODEL:2 * D_MODEL]
    mixed = jax.nn.sigmoid(m_a) * y_a + jax.nn.sigmoid(m_b) * y_b
    out_ref[...] = x + _dot(mixed.astype(BF16), w_o_ref[...])

    @pl.when(i == n_tiles - 1)
    def _():
        oa_ref[...] = tail_a
        ob_ref[...] = tail_b
        oh_ref[...] = h_last


def _ffn_kernel(x_ref, sf_ref, nf_ref, w_up_ref, w_gate_ref, fcw_ref, fcb_ref, w_down_ref,
                nfin_ref, out_ref, of_ref, extf, *, B, tT, n_tiles, final_norm):
    R = B * tT
    i = pl.program_id(0)
    nf = (CONV_F_TAPS - 1) * B

    @pl.when(i == 0)
    def _():
        extf[0:nf, :] = sf_ref[...]

    x = x_ref[...]
    hf = _rmsnorm(x, nf_ref[...]).astype(BF16)
    extf[nf:nf + R, :] = _dot(hf, w_up_ref[...])
    uc = fcw_ref[0:1, :] * extf[0:R, :]
    for k in range(1, CONV_F_TAPS):
        uc = uc + fcw_ref[k:k + 1, :] * extf[k * B:k * B + R, :]
    uc = uc + fcb_ref[...]
    tail_f = extf[R:R + nf, :]
    extf[0:nf, :] = tail_f
    gate = _dot(hf, w_gate_ref[...])
    act = (jax.nn.silu(uc) * gate).astype(BF16)
    y = x + _dot(act, w_down_ref[...])
    if final_norm:
        y = _rmsnorm(y, nfin_ref[...])
    out_ref[...] = y

    @pl.when(i == n_tiles - 1)
    def _():
        of_ref[...] = tail_f


def _resident(shape):
    nd = len(shape)
    return pl.BlockSpec(shape, lambda i: (0,) * nd, pipeline_mode=pl.Buffered(1))


def _mixer_call(x, sa, sb, sh, nm, w_in, bg, caw, w_a_out, cbw, cbb, wg, bga, bgx, lam,
                w_b_out, w_o, *, B, tT, first_pos_is_zero):
    rows = x.shape[0]
    R = B * tT
    n_tiles = rows // R
    assert n_tiles * R == rows
    na = (CONV_A_TAPS - 1) * B
    nb = (CONV_B_TAPS - 1) * B
    body = functools.partial(_mixer_kernel, B=B, tT=tT, n_tiles=n_tiles,
                             first_pos_is_zero=first_pos_is_zero)
    row_spec = pl.BlockSpec((R, D_MODEL), lambda i: (i, 0))
    small = [sa, sb, sh, nm, w_in, bg, caw, w_a_out, cbw, cbb, wg, bga, bgx, lam, w_b_out, w_o]
    return pl.pallas_call(
        body,
        grid=(n_tiles,),
        in_specs=[row_spec] + [_resident(a.shape) for a in small],
        out_specs=[row_spec,
                   pl.BlockSpec((na, D_CONV), lambda i: (0, 0)),
                   pl.BlockSpec((nb, D_RNN), lambda i: (0, 0)),
                   pl.BlockSpec((B, D_RNN), lambda i: (0, 0))],
        out_shape=[jax.ShapeDtypeStruct((rows, D_MODEL), F32),
                   jax.ShapeDtypeStruct((na, D_CONV), F32),
                   jax.ShapeDtypeStruct((nb, D_RNN), F32),
                   jax.ShapeDtypeStruct((B, D_RNN), F32)],
        scratch_shapes=[pltpu.VMEM((R + na, D_CONV), F32),
                        pltpu.VMEM((R + nb, D_RNN), F32),
                        pltpu.VMEM((R, D_RNN), F32),
                        pltpu.VMEM((R, D_RNN), F32),
                        pltpu.VMEM((B, D_RNN), F32)],
        compiler_params=pltpu.CompilerParams(dimension_semantics=("arbitrary",),
                                             vmem_limit_bytes=VMEM_LIMIT_BYTES_V7X),
        name="mixer",
    )(x, *small)


def _ffn_call(x, sf, nf_g, w_up, w_gate, fcw, fcb, w_down, nfin, *, B, tT, final_norm):
    rows = x.shape[0]
    R = B * tT
    n_tiles = rows // R
    assert n_tiles * R == rows
    nf = (CONV_F_TAPS - 1) * B
    body = functools.partial(_ffn_kernel, B=B, tT=tT, n_tiles=n_tiles, final_norm=final_norm)
    row_spec = pl.BlockSpec((R, D_MODEL), lambda i: (i, 0))
    small = [sf, nf_g, w_up, w_gate, fcw, fcb, w_down, nfin]
    return pl.pallas_call(
        body,
        grid=(n_tiles,),
        in_specs=[row_spec] + [_resident(a.shape) for a in small],
        out_specs=[row_spec, pl.BlockSpec((nf, D_FF), lambda i: (0, 0))],
        out_shape=[jax.ShapeDtypeStruct((rows, D_MODEL), F32),
                   jax.ShapeDtypeStruct((nf, D_FF), F32)],
        scratch_shapes=[pltpu.VMEM((R + nf, D_FF), F32)],
        compiler_params=pltpu.CompilerParams(dimension_semantics=("arbitrary",),
                                             vmem_limit_bytes=VMEM_LIMIT_BYTES_V7X),
        name="ffn",
    )(x, *small)


def _to_time_major(s):
    b, w, c = s.shape
    return s.transpose(1, 0, 2).reshape(w * b, c)


def _from_time_major(s, b):
    wb, c = s.shape
    return s.reshape(wb // b, b, c).transpose(1, 0, 2)


def _trunk(x, states, weights, *, B, tT, first_pos_is_zero):
    (norm_mix, norm_ffn, norm_final, w_in, b_gate, conv_a_w, w_a_out, conv_b_w, conv_b_b,
     w_gates, rg_b_a, rg_b_x, rg_lambda, w_b_out, w_o, ffn_w_up, ffn_w_gate, ffn_conv_w,
     ffn_conv_b, ffn_w_down) = weights
    bufs_a, bufs_b, hs0, bufs_f = states
    row = lambda v: v.reshape(1, -1)
    na, nb, nh, nfs = [], [], [], []
    for l in range(DEPTH):
        x, a_, b_, h_ = _mixer_call(
            x, bufs_a[l], bufs_b[l], hs0[l], row(norm_mix[l]), w_in[l], row(b_gate[l]),
            conv_a_w[l], w_a_out[l], conv_b_w[l], row(conv_b_b[l]), w_gates[l],
            row(rg_b_a[l]), row(rg_b_x[l]), row(rg_lambda[l]), w_b_out[l], w_o[l],
            B=B, tT=tT, first_pos_is_zero=first_pos_is_zero)
        x, f_ = _ffn_call(
            x, bufs_f[l], row(norm_ffn[l]), ffn_w_up[l], ffn_w_gate[l], ffn_conv_w[l],
            row(ffn_conv_b[l]), ffn_w_down[l], row(norm_final),
            B=B, tT=tT, final_norm=(l == DEPTH - 1))
        na.append(_from_time_major(a_, B))
        nb.append(_from_time_major(b_, B))
        nh.append(h_)
        nfs.append(_from_time_major(f_, B))
    return x, jnp.stack(na), jnp.stack(nb), jnp.stack(nh), jnp.stack(nfs)


def _dense_gate_weights(rg_w_a, rg_w_x):
    hpb = MXU_EDGE_V7X // RG_HEAD_DIM
    eye = jnp.eye(hpb, dtype=rg_w_a.dtype)

    def dense(w):
        w = w.reshape(DEPTH, GATE_BLOCKS, hpb, RG_HEAD_DIM, RG_HEAD_DIM)
        d = jnp.einsum('lghij,hk->lghikj', w, eye)
        return d.reshape(DEPTH, GATE_BLOCKS, MXU_EDGE_V7X, MXU_EDGE_V7X)

    return jnp.concatenate([dense(rg_w_a), dense(rg_w_x)], axis=-1).astype(BF16)


def kernel(x_prompt, x_sample, state_conv_a, state_conv_b, state_rglru, state_conv_ffn,
           meta_tokens, norm_mix, norm_ffn, norm_final, w_in, b_gate, conv_a_w, w_a_out,
           conv_b_w, conv_b_b, rg_w_a, rg_b_a, rg_w_x, rg_b_x, rg_lambda, w_b_out, w_o,
           ffn_w_up, ffn_w_gate, ffn_conv_w, ffn_conv_b, ffn_w_down):
    weights = (norm_mix, norm_ffn, norm_final, w_in.astype(BF16), b_gate, conv_a_w,
               w_a_out.astype(BF16), conv_b_w, conv_b_b, _dense_gate_weights(rg_w_a, rg_w_x),
               rg_b_a, rg_b_x, rg_lambda, w_b_out.astype(BF16), w_o.astype(BF16),
               ffn_w_up.astype(BF16), ffn_w_gate.astype(BF16), ffn_conv_w, ffn_conv_b,
               ffn_w_down.astype(BF16))

    bp, seq, _ = x_prompt.shape
    dt = x_prompt.dtype
    t_p = N_META + seq
    xp = jnp.concatenate(
        [jnp.broadcast_to(meta_tokens.astype(dt)[:, None, :], (N_META, bp, D_MODEL)),
         x_prompt.transpose(1, 0, 2)], axis=0).reshape(t_p * bp, D_MODEL)
    zeros = lambda w, c: jnp.zeros((DEPTH, w * bp, c), dt)
    p_states = (zeros(CONV_A_TAPS - 1, D_CONV), zeros(CONV_B_TAPS - 1, D_RNN),
                zeros(1, D_RNN), zeros(CONV_F_TAPS - 1, D_FF))
    yp, p_conv_a, p_conv_b, p_rglru, p_conv_ffn = _trunk(
        xp, p_states, weights, B=bp, tT=48, first_pos_is_zero=True)
    y_prompt = yp.reshape(t_p, bp, D_MODEL)[N_META:].transpose(1, 0, 2)

    bs, t_s, _ = x_sample.shape
    xs = x_sample.transpose(1, 0, 2).reshape(t_s * bs, D_MODEL)
    s_states = (jax.vmap(_to_time_major)(state_conv_a), jax.vmap(_to_time_major)(state_conv_b),
                state_rglru, jax.vmap(_to_time_major)(state_conv_ffn))
    ys, s_conv_a, s_conv_b, s_rglru, s_conv_ffn = _trunk(
        xs, s_states, weights, B=bs, tT=t_s, first_pos_is_zero=(PAST_LEN == 0))
    y_sample = ys.reshape(t_s, bs, D_MODEL).transpose(1, 0, 2)

    return (y_prompt, y_sample, p_conv_a, p_conv_b, p_rglru, p_conv_ffn,
            s_conv_a, s_conv_b, s_rglru, s_conv_ffn)
```

```python
import functools

import jax
import jax.numpy as jnp
from jax import lax
from jax.experimental import pallas as pl
from jax.experimental.pallas import tpu as pltpu

D_MODEL = 1024
DEPTH = 4
N_META = 16
D_CONV = D_MODEL
D_RNN = D_MODEL
RG_HEADS = 16
RG_HEAD_DIM = D_RNN // RG_HEADS
RG_C = 8.0
D_FF = 2816
EPS = 1e-6
PAST_LEN = 16384
CONV_A_TAPS = 3
CONV_B_TAPS = 4
CONV_F_TAPS = 3

MXU_EDGE_V7X = 256
GATE_BLOCKS = D_RNN // MXU_EDGE_V7X
VMEM_LIMIT_BYTES_V7X = 58 * 1024 * 1024
PROMPT_TILE_STEPS = 48

F32 = jnp.float32
BF16 = jnp.bfloat16


def _dot(a, b):
    return jnp.dot(a, b, preferred_element_type=F32)


def _rmsnorm(x, g):
    ms = jnp.mean(x * x, axis=-1, keepdims=True)
    return x * lax.rsqrt(ms + EPS) * g


def _softplus(z):
    return jnp.maximum(z, 0.0) + jnp.log1p(jnp.exp(-jnp.abs(z)))


def _load_state(ext, s_ref, taps, B, C):
    for k in range(taps - 1):
        ext[k * B:(k + 1) * B, :] = s_ref[:, k * C:(k + 1) * C]


def _store_state(o_ref, tail, taps, B, C):
    for k in range(taps - 1):
        o_ref[:, k * C:(k + 1) * C] = tail[k * B:(k + 1) * B, :]


def _mixer_kernel(*refs, B, tT, n_tiles, first_pos_is_zero, raw_in):
    if raw_in:
        xraw_ref, meta_ref, *refs = refs
    else:
        x_ref, *refs = refs
    (sa_ref, sb_ref, sh_ref, nm_ref, w_in_ref, bg_ref, caw_ref, w_a_out_ref, cbw_ref,
     cbb_ref, wg_ref, bga_ref, bgx_ref, lam_ref, w_b_out_ref, w_o_ref,
     out_ref, oa_ref, ob_ref, oh_ref, exta, extb, a_s, b_s, h_s, *rest) = refs
    R = B * tT
    i = pl.program_id(0)
    na = (CONV_A_TAPS - 1) * B
    nb = (CONV_B_TAPS - 1) * B

    @pl.when(i == 0)
    def _():
        _load_state(exta, sa_ref, CONV_A_TAPS, B, D_CONV)
        _load_state(extb, sb_ref, CONV_B_TAPS, B, D_RNN)
        h_s[...] = sh_ref[...]

    if raw_in:
        (xs,) = rest
        nm_rows = N_META * B

        @pl.when(i == 0)
        def _():
            xs[0:nm_rows, :] = jnp.broadcast_to(
                meta_ref[...][:, None, :], (N_META, B, D_MODEL)).reshape(nm_rows, D_MODEL)
            xs[nm_rows:R, :] = pltpu.einshape(
                "btc->tbc", xraw_ref[:, 0:tT - N_META, :]).reshape(R - nm_rows, D_MODEL)

        @pl.when(i > 0)
        def _():
            xs[...] = pltpu.einshape("btc->tbc", xraw_ref[...]).reshape(R, D_MODEL)

        x = xs[...]
    else:
        x = x_ref[...]
    hn = _rmsnorm(x, nm_ref[...]).astype(BF16)

    g_c = _dot(hn, w_in_ref[:, 1 * D_CONV:2 * D_CONV])
    h_a = _dot(hn, w_in_ref[:, 2 * D_CONV:3 * D_CONV])
    exta[na:na + R, :] = g_c * h_a
    conv_a = caw_ref[0:1, :] * exta[0:R, :]
    for k in range(1, CONV_A_TAPS):
        conv_a = conv_a + caw_ref[k:k + 1, :] * exta[k * B:k * B + R, :]
    g_b = _dot(hn, w_in_ref[:, 0:D_CONV])
    y_a = _dot((g_b * conv_a).astype(BF16), w_a_out_ref[...])
    tail_a = exta[R:R + na, :]
    exta[0:na, :] = tail_a

    o = 3 * D_CONV
    extb[nb:nb + R, :] = _dot(hn, w_in_ref[:, o:o + D_RNN])
    xc = cbw_ref[0:1, :] * extb[0:R, :]
    for k in range(1, CONV_B_TAPS):
        xc = xc + cbw_ref[k:k + 1, :] * extb[k * B:k * B + R, :]
    xc = xc + cbb_ref[...]
    tail_b = extb[R:R + nb, :]
    extb[0:nb, :] = tail_b

    xcb = xc.astype(BF16)
    neg_c_sp = -RG_C * _softplus(-lam_ref[...])
    if first_pos_is_zero:
        rows = lax.broadcasted_iota(jnp.int32, (R, MXU_EDGE_V7X), 0)
        is_first = jnp.logical_and(i == 0, rows < B)
    for j in range(GATE_BLOCKS):
        c0, c1 = j * MXU_EDGE_V7X, (j + 1) * MXU_EDGE_V7X
        g = _dot(xcb[:, c0:c1], wg_ref[j])
        r_g = jax.nn.sigmoid(g[:, :MXU_EDGE_V7X] + bga_ref[:, c0:c1])
        i_g = jax.nn.sigmoid(g[:, MXU_EDGE_V7X:] + bgx_ref[:, c0:c1])
        log_a = neg_c_sp[:, c0:c1] * r_g
        a = jnp.exp(log_a)
        s = jnp.tanh(-log_a)
        mult = jnp.sqrt(2.0 * s / (1.0 + s))
        if first_pos_is_zero:
            mult = jnp.where(is_first, 1.0, mult)
        a_s[:, c0:c1] = a
        b_s[:, c0:c1] = mult * (i_g * xc[:, c0:c1])

    def step(t, h):
        r0 = pl.multiple_of(t * B, B)
        h = a_s[pl.ds(r0, B), :] * h + b_s[pl.ds(r0, B), :]
        b_s[pl.ds(r0, B), :] = h
        return h

    h_last = lax.fori_loop(0, tT, step, h_s[...], unroll=True)
    h_s[...] = h_last

    o = 3 * D_CONV + D_RNN
    g_r = _dot(hn, w_in_ref[:, o:o + D_RNN])
    y_b = _dot((jax.nn.gelu(g_r, approximate=True) * b_s[...]).astype(BF16), w_b_out_ref[...])

    o = 3 * D_CONV + 2 * D_RNN
    m_a = _dot(hn, w_in_ref[:, o:o + D_MODEL]) + bg_ref[:, 0:D_MODEL]
    m_b = _dot(hn, w_in_ref[:, o + D_MODEL:o + 2 * D_MODEL]) + bg_ref[:, D_MODEL:2 * D_MODEL]
    mixed = jax.nn.sigmoid(m_a) * y_a + jax.nn.sigmoid(m_b) * y_b
    out_ref[...] = x + _dot(mixed.astype(BF16), w_o_ref[...])

    @pl.when(i == n_tiles - 1)
    def _():
        _store_state(oa_ref, tail_a, CONV_A_TAPS, B, D_CONV)
        _store_state(ob_ref, tail_b, CONV_B_TAPS, B, D_RNN)
        oh_ref[...] = h_last


def _ffn_kernel(x_ref, sf_ref, nf_ref, w_up_ref, w_gate_ref, fcw_ref, fcb_ref, w_down_ref,
                nfin_ref, out_ref, of_ref, extf, *rest, B, tT, n_tiles, final_norm, raw_out):
    R = B * tT
    i = pl.program_id(0)
    nf = (CONV_F_TAPS - 1) * B

    @pl.when(i == 0)
    def _():
        _load_state(extf, sf_ref, CONV_F_TAPS, B, D_FF)

    x = x_ref[...]
    hf = _rmsnorm(x, nf_ref[...]).astype(BF16)
    extf[nf:nf + R, :] = _dot(hf, w_up_ref[...])
    uc = fcw_ref[0:1, :] * extf[0:R, :]
    for k in range(1, CONV_F_TAPS):
        uc = uc + fcw_ref[k:k + 1, :] * extf[k * B:k * B + R, :]
    uc = uc + fcb_ref[...]
    tail_f = extf[R:R + nf, :]
    extf[0:nf, :] = tail_f
    gate = _dot(hf, w_gate_ref[...])
    act = (jax.nn.silu(uc) * gate).astype(BF16)
    y = x + _dot(act, w_down_ref[...])
    if final_norm:
        y = _rmsnorm(y, nfin_ref[...])

    if raw_out:
        obuf, sem = rest
        slot = lax.rem(i, 2)
        obuf[slot] = pltpu.einshape("tbc->btc", y.reshape(tT, B, D_MODEL))

        def head_copy():
            return pltpu.make_async_copy(obuf.at[0, :, N_META:tT, :],
                                         out_ref.at[:, 0:tT - N_META, :], sem.at[0])

        def tile_copy(step, slot_):
            start = pl.multiple_of(step * tT - N_META, N_META)
            return pltpu.make_async_copy(obuf.at[slot_], out_ref.at[:, pl.ds(start, tT), :],
                                         sem.at[0])

        @pl.when(i == 1)
        def _():
            head_copy().wait()

        @pl.when(i > 1)
        def _():
            tile_copy(i - 1, 1 - slot).wait()

        @pl.when(i == 0)
        def _():
            head_copy().start()

        @pl.when(i > 0)
        def _():
            tile_copy(i, slot).start()

        @pl.when(i == n_tiles - 1)
        def _():
            tile_copy(i, slot).wait()
    else:
        out_ref[...] = y

    @pl.when(i == n_tiles - 1)
    def _():
        _store_state(of_ref, tail_f, CONV_F_TAPS, B, D_FF)


def _layer_spec(arr, l):
    nd = arr.ndim - 1
    return pl.BlockSpec((None,) + arr.shape[1:], lambda i: (l,) + (0,) * nd,
                        pipeline_mode=pl.Buffered(1))


def _whole_spec(arr):
    nd = arr.ndim
    return pl.BlockSpec(arr.shape, lambda i: (0,) * nd, pipeline_mode=pl.Buffered(1))


_PARAMS = pltpu.CompilerParams(dimension_semantics=("arbitrary",),
                               vmem_limit_bytes=VMEM_LIMIT_BYTES_V7X)


def _mixer_call(x, meta, states, w, l, *, B, tT, rows, first_pos_is_zero, raw_in):
    R = B * tT
    n_tiles = rows // R
    assert n_tiles * R == rows
    na = (CONV_A_TAPS - 1) * B
    nb = (CONV_B_TAPS - 1) * B
    body = functools.partial(_mixer_kernel, B=B, tT=tT, n_tiles=n_tiles,
                             first_pos_is_zero=first_pos_is_zero, raw_in=raw_in)
    row_spec = pl.BlockSpec((R, D_MODEL), lambda i: (i, 0))
    if raw_in:
        assert tT % N_META == 0 and tT > N_META
        x_specs = [pl.BlockSpec((pl.Element(B), pl.Element(tT), pl.Element(D_MODEL)),
                                lambda i: (0, jnp.maximum(i * (tT // N_META) - 1, 0) * N_META, 0)),
                   _whole_spec(meta)]
        x_args = [x, meta]
        x_scratch = [pltpu.VMEM((R, D_MODEL), F32)]
    else:
        x_specs, x_args, x_scratch = [row_spec], [x], []
    layered = [states[0], states[1], states[2], w["norm_mix"], w["w_in"], w["b_gate"],
               w["conv_a_w"], w["w_a_out"], w["conv_b_w"], w["conv_b_b"], w["w_gates"],
               w["rg_b_a"], w["rg_b_x"], w["rg_lambda"], w["w_b_out"], w["w_o"]]
    state_spec = lambda k: pl.BlockSpec((B, k), lambda i: (0, 0))
    return pl.pallas_call(
        body,
        grid=(n_tiles,),
        in_specs=x_specs + [_layer_spec(a, l) for a in layered],
        out_specs=[row_spec, state_spec((CONV_A_TAPS - 1) * D_CONV),
                   state_spec((CONV_B_TAPS - 1) * D_RNN), state_spec(D_RNN)],
        out_shape=[jax.ShapeDtypeStruct((rows, D_MODEL), F32),
                   jax.ShapeDtypeStruct((B, (CONV_A_TAPS - 1) * D_CONV), F32),
                   jax.ShapeDtypeStruct((B, (CONV_B_TAPS - 1) * D_RNN), F32),
                   jax.ShapeDtypeStruct((B, D_RNN), F32)],
        scratch_shapes=[pltpu.VMEM((R + na, D_CONV), F32),
                        pltpu.VMEM((R + nb, D_RNN), F32),
                        pltpu.VMEM((R, D_RNN), F32),
                        pltpu.VMEM((R, D_RNN), F32),
                        pltpu.VMEM((B, D_RNN), F32)] + x_scratch,
        compiler_params=_PARAMS,
        name="mixer",
    )(*x_args, *layered)


def _ffn_call(x, state_f, w, l, *, B, tT, rows, final_norm, raw_out, seq):
    R = B * tT
    n_tiles = rows // R
    assert n_tiles * R == rows
    nf = (CONV_F_TAPS - 1) * B
    body = functools.partial(_ffn_kernel, B=B, tT=tT, n_tiles=n_tiles, final_norm=final_norm,
                             raw_out=raw_out)
    row_spec = pl.BlockSpec((R, D_MODEL), lambda i: (i, 0))
    layered = [state_f, w["norm_ffn"], w["ffn_w_up"], w["ffn_w_gate"], w["ffn_conv_w"],
               w["ffn_conv_b"], w["ffn_w_down"]]
    if raw_out:
        assert n_tiles > 2 and tT % N_META == 0 and tT > N_META
        y_spec = pl.BlockSpec(memory_space=pl.ANY)
        y_shape = jax.ShapeDtypeStruct((B, seq, D_MODEL), F32)
        y_scratch = [pltpu.VMEM((2, B, tT, D_MODEL), F32), pltpu.SemaphoreType.DMA((1,))]
    else:
        y_spec, y_shape, y_scratch = row_spec, jax.ShapeDtypeStruct((rows, D_MODEL), F32), []
    return pl.pallas_call(
        body,
        grid=(n_tiles,),
        in_specs=[row_spec] + [_layer_spec(a, l) for a in layered]
                 + [_whole_spec(w["norm_final"])],
        out_specs=[y_spec, pl.BlockSpec((B, (CONV_F_TAPS - 1) * D_FF), lambda i: (0, 0))],
        out_shape=[y_shape, jax.ShapeDtypeStruct((B, (CONV_F_TAPS - 1) * D_FF), F32)],
        scratch_shapes=[pltpu.VMEM((R + nf, D_FF), F32)] + y_scratch,
        compiler_params=_PARAMS,
        name="ffn",
    )(x, *layered, w["norm_final"])


def _trunk(x, meta, states, w, *, B, tT, rows, first_pos_is_zero, raw_io, seq):
    bufs_a, bufs_b, hs0, bufs_f = states
    na, nb, nh, nfs = [], [], [], []
    for l in range(DEPTH):
        x, a_, b_, h_ = _mixer_call(
            x, meta, (bufs_a, bufs_b, hs0), w, l, B=B, tT=tT, rows=rows,
            first_pos_is_zero=first_pos_is_zero, raw_in=(raw_io and l == 0))
        last = l == DEPTH - 1
        x, f_ = _ffn_call(x, bufs_f, w, l, B=B, tT=tT, rows=rows, final_norm=last,
                          raw_out=(raw_io and last), seq=seq)
        na.append(a_)
        nb.append(b_)
        nh.append(h_)
        nfs.append(f_)
    unflat = lambda s, taps, c: jnp.stack(s).reshape(DEPTH, B, taps - 1, c)
    return (x, unflat(na, CONV_A_TAPS, D_CONV), unflat(nb, CONV_B_TAPS, D_RNN), jnp.stack(nh),
            unflat(nfs, CONV_F_TAPS, D_FF))


def _dense_gate_weights(rg_w_a, rg_w_x):
    hpb = MXU_EDGE_V7X // RG_HEAD_DIM
    eye = jnp.eye(hpb, dtype=rg_w_a.dtype)

    def dense(wt):
        wt = wt.reshape(DEPTH, GATE_BLOCKS, hpb, RG_HEAD_DIM, RG_HEAD_DIM)
        d = jnp.einsum('lghij,hk->lghikj', wt, eye)
        return d.reshape(DEPTH, GATE_BLOCKS, MXU_EDGE_V7X, MXU_EDGE_V7X)

    return jnp.concatenate([dense(rg_w_a), dense(rg_w_x)], axis=-1).astype(BF16)


def kernel(x_prompt, x_sample, state_conv_a, state_conv_b, state_rglru, state_conv_ffn,
           meta_tokens, norm_mix, norm_ffn, norm_final, w_in, b_gate, conv_a_w, w_a_out,
           conv_b_w, conv_b_b, rg_w_a, rg_b_a, rg_w_x, rg_b_x, rg_lambda, w_b_out, w_o,
           ffn_w_up, ffn_w_gate, ffn_conv_w, ffn_conv_b, ffn_w_down):
    vec = lambda v: v.reshape(DEPTH, 1, -1)
    w = dict(
        norm_mix=vec(norm_mix), norm_ffn=vec(norm_ffn), norm_final=norm_final.reshape(1, -1),
        w_in=w_in.astype(BF16), b_gate=vec(b_gate), conv_a_w=conv_a_w,
        w_a_out=w_a_out.astype(BF16), conv_b_w=conv_b_w, conv_b_b=vec(conv_b_b),
        w_gates=_dense_gate_weights(rg_w_a, rg_w_x), rg_b_a=vec(rg_b_a), rg_b_x=vec(rg_b_x),
        rg_lambda=vec(rg_lambda), w_b_out=w_b_out.astype(BF16), w_o=w_o.astype(BF16),
        ffn_w_up=ffn_w_up.astype(BF16), ffn_w_gate=ffn_w_gate.astype(BF16),
        ffn_conv_w=ffn_conv_w, ffn_conv_b=vec(ffn_conv_b), ffn_w_down=ffn_w_down.astype(BF16))

    bp, seq, _ = x_prompt.shape
    dt = x_prompt.dtype
    zeros = lambda k: jnp.zeros((DEPTH, bp, k), dt)
    p_states = (zeros((CONV_A_TAPS - 1) * D_CONV), zeros((CONV_B_TAPS - 1) * D_RNN),
                zeros(D_RNN), zeros((CONV_F_TAPS - 1) * D_FF))
    y_prompt, p_conv_a, p_conv_b, p_rglru, p_conv_ffn = _trunk(
        x_prompt, meta_tokens.astype(dt), p_states, w, B=bp, tT=PROMPT_TILE_STEPS,
        rows=(N_META + seq) * bp, first_pos_is_zero=True, raw_io=True, seq=seq)

    bs, t_s, _ = x_sample.shape
    assert t_s == 1
    flat = lambda s: s.reshape(DEPTH, bs, -1)
    s_states = (flat(state_conv_a), flat(state_conv_b), state_rglru, flat(state_conv_ffn))
    ys, s_conv_a, s_conv_b, s_rglru, s_conv_ffn = _trunk(
        x_sample.reshape(bs, D_MODEL), None, s_states, w, B=bs, tT=t_s, rows=bs,
        first_pos_is_zero=(PAST_LEN == 0), raw_io=False, seq=t_s)
    y_sample = ys.reshape(bs, t_s, D_MODEL)

    return (y_prompt, y_sample, p_conv_a, p_conv_b, p_rglru, p_conv_ffn,
            s_conv_a, s_conv_b, s_rglru, s_conv_ffn)
```

```python
import functools

import jax
import jax.numpy as jnp
from jax import lax
from jax.experimental import pallas as pl
from jax.experimental.pallas import tpu as pltpu

D_MODEL = 1024
DEPTH = 4
N_META = 16
D_CONV = D_MODEL
D_RNN = D_MODEL
RG_HEADS = 16
RG_HEAD_DIM = D_RNN // RG_HEADS
RG_C = 8.0
D_FF = 2816
EPS = 1e-6
PAST_LEN = 16384
CONV_A_TAPS = 3
CONV_B_TAPS = 4
CONV_F_TAPS = 3

MXU_EDGE_V7X = 256
GATE_BLOCKS = D_RNN // MXU_EDGE_V7X
VMEM_LIMIT_BYTES_V7X = 58 * 1024 * 1024
PROMPT_TILE_STEPS = 48

F32 = jnp.float32
BF16 = jnp.bfloat16


def _dot(a, b):
    return jnp.dot(a, b, preferred_element_type=F32)


def _rmsnorm(x, g):
    ms = jnp.mean(x * x, axis=-1, keepdims=True)
    return x * lax.rsqrt(ms + EPS) * g


def _softplus(z):
    return jnp.maximum(z, 0.0) + jnp.log1p(jnp.exp(-jnp.abs(z)))


def _to_time_major(xb):
    b, t, c = xb.shape
    return jnp.swapaxes(xb, 0, 1).reshape(t * b, c)


def _load_state(ext, s_ref, taps, B):
    for k in range(taps - 1):
        ext[k * B:(k + 1) * B, :] = s_ref[:, k, :]


def _store_state(o_ref, tail, taps, B):
    for k in range(taps - 1):
        o_ref[:, k, :] = tail[k * B:(k + 1) * B, :]


def _mixer_tile(x, w, sc, *, B, tT, is_first):
    (nm_ref, w_in_ref, bg_ref, caw_ref, w_a_out_ref, cbw_ref, cbb_ref, wg_ref, bga_ref,
     bgx_ref, lam_ref, w_b_out_ref, w_o_ref) = w
    exta, extb, a_s, b_s, h_s, m_s = sc
    R = B * tT
    na = (CONV_A_TAPS - 1) * B
    nb = (CONV_B_TAPS - 1) * B
    hn = _rmsnorm(x, nm_ref[...]).astype(BF16)

    o_x = 3 * D_CONV
    o_m = 3 * D_CONV + 2 * D_RNN
    m_chunk = 2 * D_MODEL // GATE_BLOCKS
    neg_c_sp = -RG_C * _softplus(-lam_ref[...])
    blk = lambda j: slice(j * MXU_EDGE_V7X, (j + 1) * MXU_EDGE_V7X)
    xc_blk, g_blk, h_blk, tails_b = {}, {}, {}, {}

    def project(j):
        c = blk(j)
        extb[nb:nb + R, c] = _dot(hn, w_in_ref[:, o_x + c.start:o_x + c.stop])

    def conv_and_gate_matmul(j):
        c = blk(j)
        xc = cbw_ref[0:1, c] * extb[0:R, c]
        for k in range(1, CONV_B_TAPS):
            xc = xc + cbw_ref[k:k + 1, c] * extb[k * B:k * B + R, c]
        xc = xc + cbb_ref[:, c]
        tails_b[j] = extb[R:R + nb, c]
        extb[0:nb, c] = tails_b[j]
        xc_blk[j] = xc
        g_blk[j] = _dot(xc.astype(BF16), wg_ref[j])

    def gate_math_and_scan(j):
        c = blk(j)
        g = g_blk.pop(j)
        r_g = jax.nn.sigmoid(g[:, :MXU_EDGE_V7X] + bga_ref[:, c])
        i_g = jax.nn.sigmoid(g[:, MXU_EDGE_V7X:] + bgx_ref[:, c])
        log_a = neg_c_sp[:, c] * r_g
        s = jnp.tanh(-log_a)
        mult = jnp.sqrt(2.0 * s / (1.0 + s))
        if is_first is not None:
            mult = jnp.where(is_first, 1.0, mult)
        a_s[0:R, c] = jnp.exp(log_a)
        b_s[0:R, c] = mult * (i_g * xc_blk.pop(j))

        def step(t, h):
            r0 = pl.multiple_of(t * B, B)
            h = a_s[pl.ds(r0, B), c] * h + b_s[pl.ds(r0, B), c]
            b_s[pl.ds(r0, B), c] = h
            return h

        h_blk[j] = lax.fori_loop(0, tT, step, h_s[0:B, c], unroll=True)
        h_s[0:B, c] = h_blk[j]

    def merge_projection(j):
        mc0, mc1 = j * m_chunk, (j + 1) * m_chunk
        m_s[0:R, mc0:mc1] = _dot(hn, w_in_ref[:, o_m + mc0:o_m + mc1])

    for st in range(GATE_BLOCKS + 2):
        if st < GATE_BLOCKS:
            project(st)
        if 0 <= st - 1 < GATE_BLOCKS:
            conv_and_gate_matmul(st - 1)
            merge_projection(st - 1)
        if 0 <= st - 2 < GATE_BLOCKS:
            gate_math_and_scan(st - 2)
    tail_b = jnp.concatenate([tails_b[j] for j in range(GATE_BLOCKS)], axis=1)
    h_last = jnp.concatenate([h_blk[j] for j in range(GATE_BLOCKS)], axis=1)

    g_c = _dot(hn, w_in_ref[:, 1 * D_CONV:2 * D_CONV])
    h_a = _dot(hn, w_in_ref[:, 2 * D_CONV:3 * D_CONV])
    exta[na:na + R, :] = g_c * h_a
    conv_a = caw_ref[0:1, :] * exta[0:R, :]
    for k in range(1, CONV_A_TAPS):
        conv_a = conv_a + caw_ref[k:k + 1, :] * exta[k * B:k * B + R, :]
    g_b = _dot(hn, w_in_ref[:, 0:D_CONV])
    y_a = _dot((g_b * conv_a).astype(BF16), w_a_out_ref[...])
    tail_a = exta[R:R + na, :]
    exta[0:na, :] = tail_a

    o = 3 * D_CONV + D_RNN
    g_r = _dot(hn, w_in_ref[:, o:o + D_RNN])
    y_b = _dot((jax.nn.gelu(g_r, approximate=True) * b_s[0:R, :]).astype(BF16),
               w_b_out_ref[...])

    mixed = (jax.nn.sigmoid(m_s[0:R, 0:D_MODEL] + bg_ref[:, 0:D_MODEL]) * y_a
             + jax.nn.sigmoid(m_s[0:R, D_MODEL:2 * D_MODEL] + bg_ref[:, D_MODEL:2 * D_MODEL])
             * y_b)
    out = x + _dot(mixed.astype(BF16), w_o_ref[...])
    return out, tail_a, tail_b, h_last


def _mixer_kernel(*refs, B, tT, n_tiles, Bs, raw_in):
    if raw_in:
        xraw_ref, meta_ref, *refs = refs
    else:
        x_ref, *refs = refs
    (xs_ref, ssa_ref, ssb_ref, ssh_ref, *refs) = refs
    w, refs = refs[:13], refs[13:]
    (out_ref, oa_ref, ob_ref, oh_ref, ys_ref, soa_ref, sob_ref, soh_ref, *refs) = refs
    sc, rest = refs[:6], refs[6:]
    exta, extb, a_s, b_s, h_s, m_s = sc
    R = B * tT
    i = pl.program_id(0)
    na = (CONV_A_TAPS - 1) * B
    nb = (CONV_B_TAPS - 1) * B

    @pl.when(i == 0)
    def _():
        exta[0:na, :] = jnp.zeros((na, D_CONV), F32)
        extb[0:nb, :] = jnp.zeros((nb, D_RNN), F32)
        h_s[0:B, :] = jnp.zeros((B, D_RNN), F32)

    @pl.when(i < n_tiles)
    def _():
        if raw_in:
            (xt,) = rest
            nm_rows = N_META * B

            @pl.when(i == 0)
            def _():
                xt[0:nm_rows, :] = jnp.broadcast_to(
                    meta_ref[...][:, None, :], (N_META, B, D_MODEL)).reshape(nm_rows, D_MODEL)
                xt[nm_rows:R, :] = _to_time_major(xraw_ref[:, 0:tT - N_META, :])

            @pl.when(i > 0)
            def _():
                xt[...] = _to_time_major(xraw_ref[...])

            x = xt[...]
        else:
            x = x_ref[...]
        rows = lax.broadcasted_iota(jnp.int32, (R, MXU_EDGE_V7X), 0)
        is_first = jnp.logical_and(i == 0, rows < B)
        out, tail_a, tail_b, h_last = _mixer_tile(x, w, sc, B=B, tT=tT, is_first=is_first)
        out_ref[...] = out

        @pl.when(i == n_tiles - 1)
        def _():
            _store_state(oa_ref, tail_a, CONV_A_TAPS, B)
            _store_state(ob_ref, tail_b, CONV_B_TAPS, B)
            oh_ref[...] = h_last

    @pl.when(i == n_tiles)
    def _():
        _load_state(exta, ssa_ref, CONV_A_TAPS, Bs)
        _load_state(extb, ssb_ref, CONV_B_TAPS, Bs)
        h_s[0:Bs, :] = ssh_ref[...]
        out, tail_a, tail_b, h_last = _mixer_tile(xs_ref[...], w, sc, B=Bs, tT=1, is_first=None)
        ys_ref[...] = out
        _store_state(soa_ref, tail_a, CONV_A_TAPS, Bs)
        _store_state(sob_ref, tail_b, CONV_B_TAPS, Bs)
        soh_ref[...] = h_last


def _ffn_tile(x, w, extf, *, B, tT, final_norm):
    nf_ref, w_up_ref, w_gate_ref, fcw_ref, fcb_ref, w_down_ref, nfin_ref = w
    R = B * tT
    nf = (CONV_F_TAPS - 1) * B
    hf = _rmsnorm(x, nf_ref[...]).astype(BF16)
    extf[nf:nf + R, :] = _dot(hf, w_up_ref[...])
    uc = fcw_ref[0:1, :] * extf[0:R, :]
    for k in range(1, CONV_F_TAPS):
        uc = uc + fcw_ref[k:k + 1, :] * extf[k * B:k * B + R, :]
    uc = uc + fcb_ref[...]
    tail_f = extf[R:R + nf, :]
    extf[0:nf, :] = tail_f
    gate = _dot(hf, w_gate_ref[...])
    act = (jax.nn.silu(uc) * gate).astype(BF16)
    y = x + _dot(act, w_down_ref[...])
    if final_norm:
        y = _rmsnorm(y, nfin_ref[...])
    return y, tail_f


def _ffn_kernel(x_ref, xs_ref, ssf_ref, *refs, B, tT, n_tiles, Bs, final_norm, raw_out):
    w, refs = refs[:7], refs[7:]
    out_ref, of_ref, ys_ref, sof_ref, extf, *rest = refs
    i = pl.program_id(0)
    nf = (CONV_F_TAPS - 1) * B

    @pl.when(i == 0)
    def _():
        extf[0:nf, :] = jnp.zeros((nf, D_FF), F32)

    @pl.when(i < n_tiles)
    def _():
        y, tail_f = _ffn_tile(x_ref[...], w, extf, B=B, tT=tT, final_norm=final_norm)
        if raw_out:
            obuf, sem = rest
            slot = lax.rem(i, 2)
            obuf[slot] = jnp.swapaxes(y.reshape(tT, B, D_MODEL), 0, 1)

            def head_copy():
                return pltpu.make_async_copy(obuf.at[0, :, N_META:tT, :],
                                             out_ref.at[:, 0:tT - N_META, :], sem.at[0])

            def tile_copy(step, slot_):
                start = pl.multiple_of(step * tT - N_META, N_META)
                return pltpu.make_async_copy(
                    obuf.at[slot_], out_ref.at[:, pl.ds(start, tT), :], sem.at[0])

            @pl.when(i == 1)
            def _():
                head_copy().wait()

            @pl.when(i > 1)
            def _():
                tile_copy(i - 1, 1 - slot).wait()

            @pl.when(i == 0)
            def _():
                head_copy().start()

            @pl.when(i > 0)
            def _():
                tile_copy(i, slot).start()

            @pl.when(i == n_tiles - 1)
            def _():
                tile_copy(i, slot).wait()
        else:
            out_ref[...] = y

        @pl.when(i == n_tiles - 1)
        def _():
            _store_state(of_ref, tail_f, CONV_F_TAPS, B)

    @pl.when(i == n_tiles)
    def _():
        _load_state(extf, ssf_ref, CONV_F_TAPS, Bs)
        y, tail_f = _ffn_tile(xs_ref[...], w, extf, B=Bs, tT=1, final_norm=final_norm)
        ys_ref[...] = y
        _store_state(sof_ref, tail_f, CONV_F_TAPS, Bs)


def _layer_spec(arr, l):
    nd = arr.ndim - 1
    return pl.BlockSpec((None,) + arr.shape[1:], lambda i: (l,) + (0,) * nd,
                        pipeline_mode=pl.Buffered(1))


def _whole_spec(arr, single=True):
    nd = arr.ndim
    return pl.BlockSpec(arr.shape, lambda i: (0,) * nd,
                        pipeline_mode=pl.Buffered(1) if single else None)


def _out_whole(shape):
    nd = len(shape)
    return pl.BlockSpec(shape, lambda i: (0,) * nd)


_PARAMS = pltpu.CompilerParams(dimension_semantics=("arbitrary",),
                               vmem_limit_bytes=VMEM_LIMIT_BYTES_V7X)


def _mixer_call(x, meta, xs, s_states, w, l, *, B, tT, rows, raw_in):
    R = B * tT
    n_tiles = rows // R
    assert n_tiles * R == rows
    Bs = xs.shape[0]
    body = functools.partial(_mixer_kernel, B=B, tT=tT, n_tiles=n_tiles, Bs=Bs, raw_in=raw_in)
    last = n_tiles - 1
    row_spec = pl.BlockSpec((R, D_MODEL), lambda i: (jnp.minimum(i, last), 0))
    if raw_in:
        assert tT % N_META == 0 and tT > N_META
        spt = tT // N_META
        x_specs = [pl.BlockSpec(
            (pl.Element(B), pl.Element(tT), pl.Element(D_MODEL)),
            lambda i: (0, jnp.maximum(jnp.minimum(i, last) * spt - 1, 0) * N_META, 0)),
            _whole_spec(meta)]
        x_args = [x, meta]
        x_scratch = [pltpu.VMEM((R, D_MODEL), F32)]
    else:
        x_specs, x_args, x_scratch = [row_spec], [x], []
    ssa, ssb, ssh = s_states
    layered = [ssa, ssb, ssh, w["norm_mix"], w["w_in"], w["b_gate"], w["conv_a_w"],
               w["w_a_out"], w["conv_b_w"], w["conv_b_b"], w["w_gates"], w["rg_b_a"],
               w["rg_b_x"], w["rg_lambda"], w["w_b_out"], w["w_o"]]
    st_shapes = lambda b: [(b, CONV_A_TAPS - 1, D_CONV), (b, CONV_B_TAPS - 1, D_RNN), (b, D_RNN)]
    out_shapes = [(rows, D_MODEL)] + st_shapes(B) + [(Bs, D_MODEL)] + st_shapes(Bs)
    ext_a_rows = max(R + (CONV_A_TAPS - 1) * B, CONV_A_TAPS * Bs)
    ext_b_rows = max(R + (CONV_B_TAPS - 1) * B, CONV_B_TAPS * Bs)
    rmax = max(R, Bs)
    return pl.pallas_call(
        body,
        grid=(n_tiles + 1,),
        in_specs=x_specs + [_whole_spec(xs)] + [_layer_spec(a, l) for a in layered],
        out_specs=[row_spec] + [_out_whole(s) for s in out_shapes[1:]],
        out_shape=[jax.ShapeDtypeStruct(s, F32) for s in out_shapes],
        scratch_shapes=[pltpu.VMEM((ext_a_rows, D_CONV), F32),
                        pltpu.VMEM((ext_b_rows, D_RNN), F32),
                        pltpu.VMEM((rmax, D_RNN), F32),
                        pltpu.VMEM((rmax, D_RNN), F32),
                        pltpu.VMEM((max(B, Bs), D_RNN), F32),
                        pltpu.VMEM((rmax, 2 * D_MODEL), F32)] + x_scratch,
        compiler_params=_PARAMS,
        name="mixer",
    )(*x_args, xs, *layered)


def _ffn_call(x, xs, ssf, w, l, *, B, tT, rows, final_norm, raw_out, seq):
    R = B * tT
    n_tiles = rows // R
    assert n_tiles * R == rows
    Bs = xs.shape[0]
    body = functools.partial(_ffn_kernel, B=B, tT=tT, n_tiles=n_tiles, Bs=Bs,
                             final_norm=final_norm, raw_out=raw_out)
    last = n_tiles - 1
    row_spec = pl.BlockSpec((R, D_MODEL), lambda i: (jnp.minimum(i, last), 0))
    layered = [ssf, w["norm_ffn"], w["ffn_w_up"], w["ffn_w_gate"], w["ffn_conv_w"],
               w["ffn_conv_b"], w["ffn_w_down"]]
    if raw_out:
        assert n_tiles > 2 and tT % N_META == 0 and tT > N_META
        y_spec = pl.BlockSpec(memory_space=pl.ANY)
        y_shape = (B, seq, D_MODEL)
        y_scratch = [pltpu.VMEM((2, B, tT, D_MODEL), F32), pltpu.SemaphoreType.DMA((1,))]
    else:
        y_spec, y_shape, y_scratch = row_spec, (rows, D_MODEL), []
    out_shapes = [y_shape, (B, CONV_F_TAPS - 1, D_FF), (Bs, D_MODEL), (Bs, CONV_F_TAPS - 1, D_FF)]
    ext_f_rows = max(R + (CONV_F_TAPS - 1) * B, CONV_F_TAPS * Bs)
    return pl.pallas_call(
        body,
        grid=(n_tiles + 1,),
        in_specs=[row_spec, _whole_spec(xs)] + [_layer_spec(a, l) for a in layered]
                 + [_whole_spec(w["norm_final"])],
        out_specs=[y_spec] + [_out_whole(s) for s in out_shapes[1:]],
        out_shape=[jax.ShapeDtypeStruct(s, F32) for s in out_shapes],
        scratch_shapes=[pltpu.VMEM((ext_f_rows, D_FF), F32)] + y_scratch,
        compiler_params=_PARAMS,
        name="ffn",
    )(x, xs, *layered, w["norm_final"])


def _dense_gate_weights(rg_w_a, rg_w_x):
    hpb = MXU_EDGE_V7X // RG_HEAD_DIM
    eye = jnp.eye(hpb, dtype=rg_w_a.dtype)

    def dense(wt):
        wt = wt.reshape(DEPTH, GATE_BLOCKS, hpb, RG_HEAD_DIM, RG_HEAD_DIM)
        d = jnp.einsum('lghij,hk->lghikj', wt, eye)
        return d.reshape(DEPTH, GATE_BLOCKS, MXU_EDGE_V7X, MXU_EDGE_V7X)

    return jnp.concatenate([dense(rg_w_a), dense(rg_w_x)], axis=-1).astype(BF16)


def kernel(x_prompt, x_sample, state_conv_a, state_conv_b, state_rglru, state_conv_ffn,
           meta_tokens, norm_mix, norm_ffn, norm_final, w_in, b_gate, conv_a_w, w_a_out,
           conv_b_w, conv_b_b, rg_w_a, rg_b_a, rg_w_x, rg_b_x, rg_lambda, w_b_out, w_o,
           ffn_w_up, ffn_w_gate, ffn_conv_w, ffn_conv_b, ffn_w_down):
    assert PAST_LEN > 0
    vec = lambda v: v.reshape(DEPTH, 1, -1)
    w = dict(
        norm_mix=vec(norm_mix), norm_ffn=vec(norm_ffn), norm_final=norm_final.reshape(1, -1),
        w_in=w_in.astype(BF16), b_gate=vec(b_gate), conv_a_w=conv_a_w,
        w_a_out=w_a_out.astype(BF16), conv_b_w=conv_b_w, conv_b_b=vec(conv_b_b),
        w_gates=_dense_gate_weights(rg_w_a, rg_w_x), rg_b_a=vec(rg_b_a), rg_b_x=vec(rg_b_x),
        rg_lambda=vec(rg_lambda), w_b_out=w_b_out.astype(BF16), w_o=w_o.astype(BF16),
        ffn_w_up=ffn_w_up.astype(BF16), ffn_w_gate=ffn_w_gate.astype(BF16),
        ffn_conv_w=ffn_conv_w, ffn_conv_b=vec(ffn_conv_b), ffn_w_down=ffn_w_down.astype(BF16))

    bp, seq, _ = x_prompt.shape
    dt = x_prompt.dtype
    bs, t_s, _ = x_sample.shape
    assert t_s == 1
    rows = (N_META + seq) * bp
    x = x_prompt
    xs = x_sample.reshape(bs, D_MODEL)
    meta = meta_tokens.astype(dt)
    p_a, p_b, p_h, p_f, s_a, s_b, s_h, s_f = ([] for _ in range(8))
    for l in range(DEPTH):
        x, pa, pb, ph, xs, sa, sb, sh = _mixer_call(
            x, meta, xs, (state_conv_a, state_conv_b, state_rglru), w, l,
            B=bp, tT=PROMPT_TILE_STEPS, rows=rows, raw_in=(l == 0))
        last = l == DEPTH - 1
        x, pf, xs, sf = _ffn_call(x, xs, state_conv_ffn, w, l, B=bp, tT=PROMPT_TILE_STEPS,
                                  rows=rows, final_norm=last, raw_out=last, seq=seq)
        for acc, v in zip((p_a, p_b, p_h, p_f, s_a, s_b, s_h, s_f),
                          (pa, pb, ph, pf, sa, sb, sh, sf)):
            acc.append(v)
    y_sample = xs.reshape(bs, t_s, D_MODEL)
    return (x, y_sample, jnp.stack(p_a), jnp.stack(p_b), jnp.stack(p_h), jnp.stack(p_f),
            jnp.stack(s_a), jnp.stack(s_b), jnp.stack(s_h), jnp.stack(s_f))
```

```python
import functools

import jax
import jax.numpy as jnp
from jax import lax
from jax.experimental import pallas as pl
from jax.experimental.pallas import tpu as pltpu

D_MODEL = 1024
DEPTH = 4
N_META = 16
D_CONV = D_MODEL
D_RNN = D_MODEL
RG_HEADS = 16
RG_HEAD_DIM = D_RNN // RG_HEADS
RG_C = 8.0
D_FF = 2816
EPS = 1e-6
PAST_LEN = 16384
CONV_A_TAPS = 3
CONV_B_TAPS = 4
CONV_F_TAPS = 3

MXU_EDGE_V7X = 256
GATE_BLOCKS = D_RNN // MXU_EDGE_V7X
VMEM_LIMIT_BYTES_V7X = 58 * 1024 * 1024
PROMPT_TILE_STEPS = 48

F32 = jnp.float32
BF16 = jnp.bfloat16


def _dot(a, b):
    return jnp.dot(a, b, preferred_element_type=F32)


def _rmsnorm(x, g):
    ms = jnp.mean(x * x, axis=-1, keepdims=True)
    return x * lax.rsqrt(ms + EPS) * g


def _sigmoid(z):
    return 0.5 * jnp.tanh(0.5 * z) + 0.5


def _softplus(z):
    return jnp.maximum(z, 0.0) + jnp.log1p(jnp.exp(-jnp.abs(z)))


def _to_time_major(xb):
    b, t, c = xb.shape
    return jnp.swapaxes(xb, 0, 1).reshape(t * b, c)


def _load_state(ext, s_ref, taps, B):
    for k in range(taps - 1):
        ext[k * B:(k + 1) * B, :] = s_ref[:, k, :]


def _store_state(o_ref, tail, taps, B):
    for k in range(taps - 1):
        o_ref[:, k, :] = tail[k * B:(k + 1) * B, :]


def _mixer_tile(x, w, sc, *, B, tT, is_first):
    (nm_ref, w_in_ref, bg_ref, caw_ref, w_a_out_ref, cbw_ref, cbb_ref, wg_ref, bga_ref,
     bgx_ref, lam_ref, w_b_out_ref, w_o_ref) = w
    exta, extb, a_s, b_s, h_s = sc
    R = B * tT
    na = (CONV_A_TAPS - 1) * B
    nb = (CONV_B_TAPS - 1) * B
    hn = _rmsnorm(x, nm_ref[...]).astype(BF16)

    o_x = 3 * D_CONV
    neg_c_sp = -RG_C * _softplus(-lam_ref[...])
    blk = lambda j: slice(j * MXU_EDGE_V7X, (j + 1) * MXU_EDGE_V7X)
    extb[nb:nb + R, :] = _dot(hn, w_in_ref[:, o_x:o_x + D_RNN])
    xc = cbw_ref[0:1, :] * extb[0:R, :]
    for k in range(1, CONV_B_TAPS):
        xc = xc + cbw_ref[k:k + 1, :] * extb[k * B:k * B + R, :]
    xc = xc + cbb_ref[...]
    tail_b = extb[R:R + nb, :]
    extb[0:nb, :] = tail_b
    xcb = xc.astype(BF16)
    gates = [_dot(xcb[:, blk(j)], wg_ref[j]) for j in range(GATE_BLOCKS)]

    wide_a = _dot(hn, w_in_ref[:, 0:3 * D_CONV])

    h_blk = []
    for j in range(GATE_BLOCKS):
        c = blk(j)
        g = gates[j]
        r_g = _sigmoid(g[:, :MXU_EDGE_V7X] + bga_ref[:, c])
        i_g = _sigmoid(g[:, MXU_EDGE_V7X:] + bgx_ref[:, c])
        log_a = neg_c_sp[:, c] * r_g
        s = jnp.tanh(-log_a)
        mult = jnp.sqrt(2.0 * s / (1.0 + s))
        if is_first is not None:
            mult = jnp.where(is_first, 1.0, mult)
        a_s[0:R, c] = jnp.exp(log_a)
        b_s[0:R, c] = mult * (i_g * xc[:, c])

        def step(t, h, c=c):
            r0 = pl.multiple_of(t * B, B)
            h = a_s[pl.ds(r0, B), c] * h + b_s[pl.ds(r0, B), c]
            b_s[pl.ds(r0, B), c] = h
            return h

        h_blk.append(lax.fori_loop(0, tT, step, h_s[0:B, c], unroll=True))
        h_s[0:B, c] = h_blk[j]
    h_last = jnp.concatenate(h_blk, axis=1)

    o_r = 3 * D_CONV + D_RNN
    wide_b = _dot(hn, w_in_ref[:, o_r:o_r + 3 * D_MODEL])

    exta[na:na + R, :] = wide_a[:, D_CONV:2 * D_CONV] * wide_a[:, 2 * D_CONV:3 * D_CONV]
    conv_a = caw_ref[0:1, :] * exta[0:R, :]
    for k in range(1, CONV_A_TAPS):
        conv_a = conv_a + caw_ref[k:k + 1, :] * exta[k * B:k * B + R, :]
    y_a = _dot((wide_a[:, 0:D_CONV] * conv_a).astype(BF16), w_a_out_ref[...])
    tail_a = exta[R:R + na, :]
    exta[0:na, :] = tail_a

    y_b = _dot((jax.nn.gelu(wide_b[:, 0:D_RNN], approximate=True) * b_s[0:R, :]).astype(BF16),
               w_b_out_ref[...])

    mixed = (_sigmoid(wide_b[:, D_RNN:D_RNN + D_MODEL] + bg_ref[:, 0:D_MODEL]) * y_a
             + _sigmoid(wide_b[:, D_RNN + D_MODEL:] + bg_ref[:, D_MODEL:2 * D_MODEL]) * y_b)
    out = x + _dot(mixed.astype(BF16), w_o_ref[...])
    return out, tail_a, tail_b, h_last


def _mixer_kernel(*refs, B, tT, n_tiles, Bs, raw_in):
    if raw_in:
        xraw_ref, meta_ref, *refs = refs
    else:
        x_ref, *refs = refs
    (xs_ref, ssa_ref, ssb_ref, ssh_ref, *refs) = refs
    w, refs = refs[:13], refs[13:]
    (out_ref, oa_ref, ob_ref, oh_ref, ys_ref, soa_ref, sob_ref, soh_ref, *refs) = refs
    sc, rest = refs[:5], refs[5:]
    exta, extb, a_s, b_s, h_s = sc
    R = B * tT
    i = pl.program_id(0)
    na = (CONV_A_TAPS - 1) * B
    nb = (CONV_B_TAPS - 1) * B

    @pl.when(i == 0)
    def _():
        exta[0:na, :] = jnp.zeros((na, D_CONV), F32)
        extb[0:nb, :] = jnp.zeros((nb, D_RNN), F32)
        h_s[0:B, :] = jnp.zeros((B, D_RNN), F32)

    @pl.when(i < n_tiles)
    def _():
        if raw_in:
            (xt,) = rest
            nm_rows = N_META * B

            @pl.when(i == 0)
            def _():
                xt[0:nm_rows, :] = jnp.broadcast_to(
                    meta_ref[...][:, None, :], (N_META, B, D_MODEL)).reshape(nm_rows, D_MODEL)
                xt[nm_rows:R, :] = _to_time_major(xraw_ref[:, 0:tT - N_META, :])

            @pl.when(i > 0)
            def _():
                xt[...] = _to_time_major(xraw_ref[...])

            x = xt[...]
        else:
            x = x_ref[...]
        rows = lax.broadcasted_iota(jnp.int32, (R, MXU_EDGE_V7X), 0)
        is_first = jnp.logical_and(i == 0, rows < B)
        out, tail_a, tail_b, h_last = _mixer_tile(x, w, sc, B=B, tT=tT, is_first=is_first)
        out_ref[...] = out

        @pl.when(i == n_tiles - 1)
        def _():
            _store_state(oa_ref, tail_a, CONV_A_TAPS, B)
            _store_state(ob_ref, tail_b, CONV_B_TAPS, B)
            oh_ref[...] = h_last

    @pl.when(i == n_tiles)
    def _():
        _load_state(exta, ssa_ref, CONV_A_TAPS, Bs)
        _load_state(extb, ssb_ref, CONV_B_TAPS, Bs)
        h_s[0:Bs, :] = ssh_ref[...]
        out, tail_a, tail_b, h_last = _mixer_tile(xs_ref[...], w, sc, B=Bs, tT=1, is_first=None)
        ys_ref[...] = out
        _store_state(soa_ref, tail_a, CONV_A_TAPS, Bs)
        _store_state(sob_ref, tail_b, CONV_B_TAPS, Bs)
        soh_ref[...] = h_last


def _ffn_tile(x, w, extf, *, B, tT, final_norm):
    nf_ref, w_up_ref, w_gate_ref, fcw_ref, fcb_ref, w_down_ref, nfin_ref = w
    R = B * tT
    nf = (CONV_F_TAPS - 1) * B
    hf = _rmsnorm(x, nf_ref[...]).astype(BF16)
    extf[nf:nf + R, :] = _dot(hf, w_up_ref[...])
    uc = fcw_ref[0:1, :] * extf[0:R, :]
    for k in range(1, CONV_F_TAPS):
        uc = uc + fcw_ref[k:k + 1, :] * extf[k * B:k * B + R, :]
    uc = uc + fcb_ref[...]
    tail_f = extf[R:R + nf, :]
    extf[0:nf, :] = tail_f
    gate = _dot(hf, w_gate_ref[...])
    act = (uc * _sigmoid(uc) * gate).astype(BF16)
    y = x + _dot(act, w_down_ref[...])
    if final_norm:
        y = _rmsnorm(y, nfin_ref[...])
    return y, tail_f


def _ffn_kernel(x_ref, xs_ref, ssf_ref, *refs, B, tT, n_tiles, Bs, final_norm, raw_out):
    w, refs = refs[:7], refs[7:]
    out_ref, of_ref, ys_ref, sof_ref, extf, *rest = refs
    i = pl.program_id(0)
    nf = (CONV_F_TAPS - 1) * B

    @pl.when(i == 0)
    def _():
        extf[0:nf, :] = jnp.zeros((nf, D_FF), F32)

    @pl.when(i < n_tiles)
    def _():
        y, tail_f = _ffn_tile(x_ref[...], w, extf, B=B, tT=tT, final_norm=final_norm)
        if raw_out:
            obuf, sem = rest
            slot = lax.rem(i, 2)
            obuf[slot] = jnp.swapaxes(y.reshape(tT, B, D_MODEL), 0, 1)

            def head_copy():
                return pltpu.make_async_copy(obuf.at[0, :, N_META:tT, :],
                                             out_ref.at[:, 0:tT - N_META, :], sem.at[0])

            def tile_copy(step, slot_):
                start = pl.multiple_of(step * tT - N_META, N_META)
                return pltpu.make_async_copy(
                    obuf.at[slot_], out_ref.at[:, pl.ds(start, tT), :], sem.at[0])

            @pl.when(i == 1)
            def _():
                head_copy().wait()

            @pl.when(i > 1)
            def _():
                tile_copy(i - 1, 1 - slot).wait()

            @pl.when(i == 0)
            def _():
                head_copy().start()

            @pl.when(i > 0)
            def _():
                tile_copy(i, slot).start()

            @pl.when(i == n_tiles - 1)
            def _():
                tile_copy(i, slot).wait()
        else:
            out_ref[...] = y

        @pl.when(i == n_tiles - 1)
        def _():
            _store_state(of_ref, tail_f, CONV_F_TAPS, B)

    @pl.when(i == n_tiles)
    def _():
        _load_state(extf, ssf_ref, CONV_F_TAPS, Bs)
        y, tail_f = _ffn_tile(xs_ref[...], w, extf, B=Bs, tT=1, final_norm=final_norm)
        ys_ref[...] = y
        _store_state(sof_ref, tail_f, CONV_F_TAPS, Bs)


def _layer_spec(arr, l):
    nd = arr.ndim - 1
    return pl.BlockSpec((None,) + arr.shape[1:], lambda i: (l,) + (0,) * nd,
                        pipeline_mode=pl.Buffered(1))


def _whole_spec(arr):
    nd = arr.ndim
    return pl.BlockSpec(arr.shape, lambda i: (0,) * nd, pipeline_mode=pl.Buffered(1))


def _out_whole(shape):
    nd = len(shape)
    return pl.BlockSpec(shape, lambda i: (0,) * nd)


_PARAMS = pltpu.CompilerParams(dimension_semantics=("arbitrary",),
                               vmem_limit_bytes=VMEM_LIMIT_BYTES_V7X)


def _mixer_call(x, meta, xs, s_states, w, l, *, B, tT, rows, raw_in):
    R = B * tT
    n_tiles = rows // R
    assert n_tiles * R == rows
    Bs = xs.shape[0]
    body = functools.partial(_mixer_kernel, B=B, tT=tT, n_tiles=n_tiles, Bs=Bs, raw_in=raw_in)
    last = n_tiles - 1
    row_spec = pl.BlockSpec((R, D_MODEL), lambda i: (jnp.minimum(i, last), 0))
    if raw_in:
        assert tT % N_META == 0 and tT > N_META
        spt = tT // N_META
        x_specs = [pl.BlockSpec(
            (pl.Element(B), pl.Element(tT), pl.Element(D_MODEL)),
            lambda i: (0, jnp.maximum(jnp.minimum(i, last) * spt - 1, 0) * N_META, 0)),
            _whole_spec(meta)]
        x_args = [x, meta]
        x_scratch = [pltpu.VMEM((R, D_MODEL), F32)]
    else:
        x_specs, x_args, x_scratch = [row_spec], [x], []
    ssa, ssb, ssh = s_states
    layered = [ssa, ssb, ssh, w["norm_mix"], w["w_in"], w["b_gate"], w["conv_a_w"],
               w["w_a_out"], w["conv_b_w"], w["conv_b_b"], w["w_gates"], w["rg_b_a"],
               w["rg_b_x"], w["rg_lambda"], w["w_b_out"], w["w_o"]]
    st_shapes = lambda b: [(b, CONV_A_TAPS - 1, D_CONV), (b, CONV_B_TAPS - 1, D_RNN), (b, D_RNN)]
    out_shapes = [(rows, D_MODEL)] + st_shapes(B) + [(Bs, D_MODEL)] + st_shapes(Bs)
    ext_a_rows = max(R + (CONV_A_TAPS - 1) * B, CONV_A_TAPS * Bs)
    ext_b_rows = max(R + (CONV_B_TAPS - 1) * B, CONV_B_TAPS * Bs)
    rmax = max(R, Bs)
    return pl.pallas_call(
        body,
        grid=(n_tiles + 1,),
        in_specs=x_specs + [_whole_spec(xs)] + [_layer_spec(a, l) for a in layered],
        out_specs=[row_spec] + [_out_whole(s) for s in out_shapes[1:]],
        out_shape=[jax.ShapeDtypeStruct(s, F32) for s in out_shapes],
        scratch_shapes=[pltpu.VMEM((ext_a_rows, D_CONV), F32),
                        pltpu.VMEM((ext_b_rows, D_RNN), F32),
                        pltpu.VMEM((rmax, D_RNN), F32),
                        pltpu.VMEM((rmax, D_RNN), F32),
                        pltpu.VMEM((max(B, Bs), D_RNN), F32)] + x_scratch,
        compiler_params=_PARAMS,
        name="mixer",
    )(*x_args, xs, *layered)


def _ffn_call(x, xs, ssf, w, l, *, B, tT, rows, final_norm, raw_out, seq):
    R = B * tT
    n_tiles = rows // R
    assert n_tiles * R == rows
    Bs = xs.shape[0]
    body = functools.partial(_ffn_kernel, B=B, tT=tT, n_tiles=n_tiles, Bs=Bs,
                             final_norm=final_norm, raw_out=raw_out)
    last = n_tiles - 1
    row_spec = pl.BlockSpec((R, D_MODEL), lambda i: (jnp.minimum(i, last), 0))
    layered = [ssf, w["norm_ffn"], w["ffn_w_up"], w["ffn_w_gate"], w["ffn_conv_w"],
               w["ffn_conv_b"], w["ffn_w_down"]]
    if raw_out:
        assert n_tiles > 2 and tT % N_META == 0 and tT > N_META
        y_spec = pl.BlockSpec(memory_space=pl.ANY)
        y_shape = (B, seq, D_MODEL)
        y_scratch = [pltpu.VMEM((2, B, tT, D_MODEL), F32), pltpu.SemaphoreType.DMA((1,))]
    else:
        y_spec, y_shape, y_scratch = row_spec, (rows, D_MODEL), []
    out_shapes = [y_shape, (B, CONV_F_TAPS - 1, D_FF), (Bs, D_MODEL), (Bs, CONV_F_TAPS - 1, D_FF)]
    ext_f_rows = max(R + (CONV_F_TAPS - 1) * B, CONV_F_TAPS * Bs)
    return pl.pallas_call(
        body,
        grid=(n_tiles + 1,),
        in_specs=[row_spec, _whole_spec(xs)] + [_layer_spec(a, l) for a in layered]
                 + [_whole_spec(w["norm_final"])],
        out_specs=[y_spec] + [_out_whole(s) for s in out_shapes[1:]],
        out_shape=[jax.ShapeDtypeStruct(s, F32) for s in out_shapes],
        scratch_shapes=[pltpu.VMEM((ext_f_rows, D_FF), F32)] + y_scratch,
        compiler_params=_PARAMS,
        name="ffn",
    )(x, xs, *layered, w["norm_final"])


def _dense_gate_weights(rg_w_a, rg_w_x):
    hpb = MXU_EDGE_V7X // RG_HEAD_DIM
    idx = jnp.arange(MXU_EDGE_V7X) // RG_HEAD_DIM
    on_diag = idx[:, None] == idx[None, :]

    def dense(wt):
        wt = wt.reshape(DEPTH, GATE_BLOCKS, hpb, RG_HEAD_DIM, RG_HEAD_DIM)
        wide = wt.transpose(0, 1, 3, 2, 4).reshape(DEPTH, GATE_BLOCKS, RG_HEAD_DIM, MXU_EDGE_V7X)
        return jnp.where(on_diag, jnp.tile(wide, (1, 1, hpb, 1)), 0.0)

    return jnp.concatenate([dense(rg_w_a), dense(rg_w_x)], axis=-1).astype(BF16)


def kernel(x_prompt, x_sample, state_conv_a, state_conv_b, state_rglru, state_conv_ffn,
           meta_tokens, norm_mix, norm_ffn, norm_final, w_in, b_gate, conv_a_w, w_a_out,
           conv_b_w, conv_b_b, rg_w_a, rg_b_a, rg_w_x, rg_b_x, rg_lambda, w_b_out, w_o,
           ffn_w_up, ffn_w_gate, ffn_conv_w, ffn_conv_b, ffn_w_down):
    assert PAST_LEN > 0
    vec = lambda v: v.reshape(DEPTH, 1, -1)
    w = dict(
        norm_mix=vec(norm_mix), norm_ffn=vec(norm_ffn), norm_final=norm_final.reshape(1, -1),
        w_in=w_in.astype(BF16), b_gate=vec(b_gate), conv_a_w=conv_a_w,
        w_a_out=w_a_out.astype(BF16), conv_b_w=conv_b_w, conv_b_b=vec(conv_b_b),
        w_gates=_dense_gate_weights(rg_w_a, rg_w_x), rg_b_a=vec(rg_b_a), rg_b_x=vec(rg_b_x),
        rg_lambda=vec(rg_lambda), w_b_out=w_b_out.astype(BF16), w_o=w_o.astype(BF16),
        ffn_w_up=ffn_w_up.astype(BF16), ffn_w_gate=ffn_w_gate.astype(BF16),
        ffn_conv_w=ffn_conv_w, ffn_conv_b=vec(ffn_conv_b), ffn_w_down=ffn_w_down.astype(BF16))

    bp, seq, _ = x_prompt.shape
    dt = x_prompt.dtype
    bs, t_s, _ = x_sample.shape
    assert t_s == 1
    rows = (N_META + seq) * bp
    x = x_prompt
    xs = x_sample.reshape(bs, D_MODEL)
    meta = meta_tokens.astype(dt)
    p_a, p_b, p_h, p_f, s_a, s_b, s_h, s_f = ([] for _ in range(8))
    for l in range(DEPTH):
        x, pa, pb, ph, xs, sa, sb, sh = _mixer_call(
            x, meta, xs, (state_conv_a, state_conv_b, state_rglru), w, l,
            B=bp, tT=PROMPT_TILE_STEPS, rows=rows, raw_in=(l == 0))
        last = l == DEPTH - 1
        x, pf, xs, sf = _ffn_call(x, xs, state_conv_ffn, w, l, B=bp, tT=PROMPT_TILE_STEPS,
                                  rows=rows, final_norm=last, raw_out=last, seq=seq)
        for acc, v in zip((p_a, p_b, p_h, p_f, s_a, s_b, s_h, s_f),
                          (pa, pb, ph, pf, sa, sb, sh, sf)):
            acc.append(v)
    y_sample = xs.reshape(bs, t_s, D_MODEL)
    return (x, y_sample, jnp.stack(p_a), jnp.stack(p_b), jnp.stack(p_h), jnp.stack(p_f),
            jnp.stack(s_a), jnp.stack(s_b), jnp.stack(s_h), jnp.stack(s_f))
```

```python
import functools

import jax
import jax.numpy as jnp
from jax import lax
from jax.experimental import pallas as pl
from jax.experimental.pallas import tpu as pltpu

D_MODEL = 1024
DEPTH = 4
N_META = 16
D_CONV = D_MODEL
D_RNN = D_MODEL
RG_HEADS = 16
RG_HEAD_DIM = D_RNN // RG_HEADS
RG_C = 8.0
D_FF = 2816
EPS = 1e-6
PAST_LEN = 16384
CONV_A_TAPS = 3
CONV_B_TAPS = 4
CONV_F_TAPS = 3

MXU_EDGE_V7X = 256
GATE_BLOCKS = D_RNN // MXU_EDGE_V7X
VMEM_LIMIT_BYTES_V7X = 58 * 1024 * 1024
PROMPT_TILE_STEPS = 48
FFN_TILE_STEPS = 86

F32 = jnp.float32
BF16 = jnp.bfloat16


def _dot(a, b):
    return jnp.dot(a, b, preferred_element_type=F32)


def _rmsnorm(x, g):
    ms = jnp.mean(x * x, axis=-1, keepdims=True)
    return x * lax.rsqrt(ms + EPS) * g


def _sigmoid(z):
    return 0.5 * jnp.tanh(0.5 * z) + 0.5


def _softplus(z):
    return jnp.maximum(z, 0.0) + jnp.log1p(jnp.exp(-jnp.abs(z)))


def _to_time_major(xb):
    b, t, c = xb.shape
    return jnp.swapaxes(xb, 0, 1).reshape(t * b, c)


def _load_state(ext, s_ref, taps, B):
    for k in range(taps - 1):
        ext[k * B:(k + 1) * B, :] = s_ref[:, k, :]


def _store_state(o_ref, tail, taps, B):
    for k in range(taps - 1):
        o_ref[:, k, :] = tail[k * B:(k + 1) * B, :]


def _mixer_tile(x, w, sc, *, B, tT, is_first):
    (nm_ref, w_in_ref, bg_ref, caw_ref, w_a_out_ref, cbw_ref, cbb_ref, wg_ref, bga_ref,
     bgx_ref, lam_ref, w_b_out_ref, w_o_ref) = w
    exta, extb, a_s, b_s, h_s, g_s = sc
    R = B * tT
    na = (CONV_A_TAPS - 1) * B
    nb = (CONV_B_TAPS - 1) * B
    hn = _rmsnorm(x, nm_ref[...]).astype(BF16)

    o_x = 3 * D_CONV
    neg_c_sp = -RG_C * _softplus(-lam_ref[...])
    blk = lambda j: slice(j * MXU_EDGE_V7X, (j + 1) * MXU_EDGE_V7X)
    extb[nb:nb + R, :] = _dot(hn, w_in_ref[:, o_x:o_x + D_RNN])
    wide_a = _dot(hn, w_in_ref[:, 0:3 * D_CONV])
    xc = cbw_ref[0:1, :] * extb[0:R, :]
    for k in range(1, CONV_B_TAPS):
        xc = xc + cbw_ref[k:k + 1, :] * extb[k * B:k * B + R, :]
    xc = xc + cbb_ref[...]
    tail_b = extb[R:R + nb, :]
    extb[0:nb, :] = tail_b
    xcb = xc.astype(BF16)
    for j in range(GATE_BLOCKS):
        g_s[0:R, 2 * j * MXU_EDGE_V7X:2 * (j + 1) * MXU_EDGE_V7X] = _dot(xcb[:, blk(j)], wg_ref[j])
    o_r = 3 * D_CONV + D_RNN
    wide_b = _dot(hn, w_in_ref[:, o_r:o_r + 3 * D_MODEL])

    h_blk = []
    for j in range(GATE_BLOCKS):
        c = blk(j)
        g = g_s[0:R, 2 * j * MXU_EDGE_V7X:2 * (j + 1) * MXU_EDGE_V7X]
        r_g = _sigmoid(g[:, :MXU_EDGE_V7X] + bga_ref[:, c])
        i_g = _sigmoid(g[:, MXU_EDGE_V7X:] + bgx_ref[:, c])
        log_a = neg_c_sp[:, c] * r_g
        s = jnp.tanh(-log_a)
        mult = jnp.sqrt(2.0 * s / (1.0 + s))
        if is_first is not None:
            mult = jnp.where(is_first, 1.0, mult)
        a_s[0:R, c] = jnp.exp(log_a)
        b_s[0:R, c] = mult * (i_g * xc[:, c])

        def step(t, h, c=c):
            r0 = pl.multiple_of(t * B, B)
            h = a_s[pl.ds(r0, B), c] * h + b_s[pl.ds(r0, B), c]
            b_s[pl.ds(r0, B), c] = h
            return h

        h_blk.append(lax.fori_loop(0, tT, step, h_s[0:B, c], unroll=True))
        h_s[0:B, c] = h_blk[j]
    h_last = jnp.concatenate(h_blk, axis=1)

    exta[na:na + R, :] = wide_a[:, D_CONV:2 * D_CONV] * wide_a[:, 2 * D_CONV:3 * D_CONV]
    conv_a = caw_ref[0:1, :] * exta[0:R, :]
    for k in range(1, CONV_A_TAPS):
        conv_a = conv_a + caw_ref[k:k + 1, :] * exta[k * B:k * B + R, :]
    y_a = _dot((wide_a[:, 0:D_CONV] * conv_a).astype(BF16), w_a_out_ref[...])
    tail_a = exta[R:R + na, :]
    exta[0:na, :] = tail_a

    y_b = _dot((jax.nn.gelu(wide_b[:, 0:D_RNN], approximate=True) * b_s[0:R, :]).astype(BF16),
               w_b_out_ref[...])

    mixed = (_sigmoid(wide_b[:, D_RNN:D_RNN + D_MODEL] + bg_ref[:, 0:D_MODEL]) * y_a
             + _sigmoid(wide_b[:, D_RNN + D_MODEL:] + bg_ref[:, D_MODEL:2 * D_MODEL]) * y_b)
    out = x + _dot(mixed.astype(BF16), w_o_ref[...])
    return out, tail_a, tail_b, h_last


def _mixer_kernel(*refs, B, tT, n_tiles, Bs, raw_in):
    if raw_in:
        xraw_ref, meta_ref, *refs = refs
    else:
        x_ref, *refs = refs
    (xs_ref, ssa_ref, ssb_ref, ssh_ref, *refs) = refs
    w, refs = refs[:13], refs[13:]
    (out_ref, oa_ref, ob_ref, oh_ref, ys_ref, soa_ref, sob_ref, soh_ref, *refs) = refs
    sc, rest = refs[:6], refs[6:]
    exta, extb, a_s, b_s, h_s, _ = sc
    R = B * tT
    i = pl.program_id(0)
    na = (CONV_A_TAPS - 1) * B
    nb = (CONV_B_TAPS - 1) * B

    @pl.when(i == 0)
    def _():
        exta[0:na, :] = jnp.zeros((na, D_CONV), F32)
        extb[0:nb, :] = jnp.zeros((nb, D_RNN), F32)
        h_s[0:B, :] = jnp.zeros((B, D_RNN), F32)

    @pl.when(i < n_tiles)
    def _():
        if raw_in:
            (xt,) = rest
            nm_rows = N_META * B

            @pl.when(i == 0)
            def _():
                xt[0:nm_rows, :] = jnp.broadcast_to(
                    meta_ref[...][:, None, :], (N_META, B, D_MODEL)).reshape(nm_rows, D_MODEL)
                xt[nm_rows:R, :] = _to_time_major(xraw_ref[:, 0:tT - N_META, :])

            @pl.when(i > 0)
            def _():
                xt[...] = _to_time_major(xraw_ref[...])

            x = xt[...]
        else:
            x = x_ref[...]
        rows = lax.broadcasted_iota(jnp.int32, (R, MXU_EDGE_V7X), 0)
        is_first = jnp.logical_and(i == 0, rows < B)
        out, tail_a, tail_b, h_last = _mixer_tile(x, w, sc, B=B, tT=tT, is_first=is_first)
        out_ref[...] = out

        @pl.when(i == n_tiles - 1)
        def _():
            _store_state(oa_ref, tail_a, CONV_A_TAPS, B)
            _store_state(ob_ref, tail_b, CONV_B_TAPS, B)
            oh_ref[...] = h_last

    @pl.when(i == n_tiles)
    def _():
        _load_state(exta, ssa_ref, CONV_A_TAPS, Bs)
        _load_state(extb, ssb_ref, CONV_B_TAPS, Bs)
        h_s[0:Bs, :] = ssh_ref[...]
        out, tail_a, tail_b, h_last = _mixer_tile(xs_ref[...], w, sc, B=Bs, tT=1, is_first=None)
        ys_ref[...] = out
        _store_state(soa_ref, tail_a, CONV_A_TAPS, Bs)
        _store_state(sob_ref, tail_b, CONV_B_TAPS, Bs)
        soh_ref[...] = h_last


def _ffn_tile(x, w, extf, *, B, tT, final_norm):
    nf_ref, w_up_ref, w_gate_ref, fcw_ref, fcb_ref, w_down_ref, nfin_ref = w
    R = B * tT
    nf = (CONV_F_TAPS - 1) * B
    hf = _rmsnorm(x, nf_ref[...]).astype(BF16)
    extf[nf:nf + R, :] = _dot(hf, w_up_ref[...])
    uc = fcw_ref[0:1, :] * extf[0:R, :]
    for k in range(1, CONV_F_TAPS):
        uc = uc + fcw_ref[k:k + 1, :] * extf[k * B:k * B + R, :]
    uc = uc + fcb_ref[...]
    tail_f = extf[R:R + nf, :]
    extf[0:nf, :] = tail_f
    gate = _dot(hf, w_gate_ref[...])
    act = (uc * _sigmoid(uc) * gate).astype(BF16)
    y = x + _dot(act, w_down_ref[...])
    if final_norm:
        y = _rmsnorm(y, nfin_ref[...])
    return y, tail_f


def _ffn_kernel(x_ref, xs_ref, ssf_ref, *refs, B, tT, n_tiles, Bs, final_norm, raw_out):
    w, refs = refs[:7], refs[7:]
    out_ref, of_ref, ys_ref, sof_ref, extf, *rest = refs
    i = pl.program_id(0)
    nf = (CONV_F_TAPS - 1) * B

    @pl.when(i == 0)
    def _():
        extf[0:nf, :] = jnp.zeros((nf, D_FF), F32)

    @pl.when(i < n_tiles)
    def _():
        y, tail_f = _ffn_tile(x_ref[...], w, extf, B=B, tT=tT, final_norm=final_norm)
        if raw_out:
            obuf, sem = rest
            slot = lax.rem(i, 2)
            obuf[slot] = jnp.swapaxes(y.reshape(tT, B, D_MODEL), 0, 1)

            def head_copy():
                return pltpu.make_async_copy(obuf.at[0, :, N_META:tT, :],
                                             out_ref.at[:, 0:tT - N_META, :], sem.at[0])

            def tile_copy(step, slot_):
                start = pl.multiple_of(step * tT - N_META, N_META)
                return pltpu.make_async_copy(
                    obuf.at[slot_], out_ref.at[:, pl.ds(start, tT), :], sem.at[0])

            @pl.when(i == 1)
            def _():
                head_copy().wait()

            @pl.when(i > 1)
            def _():
                tile_copy(i - 1, 1 - slot).wait()

            @pl.when(i == 0)
            def _():
                head_copy().start()

            @pl.when(i > 0)
            def _():
                tile_copy(i, slot).start()

            @pl.when(i == n_tiles - 1)
            def _():
                tile_copy(i, slot).wait()
        else:
            out_ref[...] = y

        @pl.when(i == n_tiles - 1)
        def _():
            _store_state(of_ref, tail_f, CONV_F_TAPS, B)

    @pl.when(i == n_tiles)
    def _():
        _load_state(extf, ssf_ref, CONV_F_TAPS, Bs)
        y, tail_f = _ffn_tile(xs_ref[...], w, extf, B=Bs, tT=1, final_norm=final_norm)
        ys_ref[...] = y
        _store_state(sof_ref, tail_f, CONV_F_TAPS, Bs)


def _layer_spec(arr, l):
    nd = arr.ndim - 1
    return pl.BlockSpec((None,) + arr.shape[1:], lambda i: (l,) + (0,) * nd,
                        pipeline_mode=pl.Buffered(1))


def _whole_spec(arr):
    nd = arr.ndim
    return pl.BlockSpec(arr.shape, lambda i: (0,) * nd, pipeline_mode=pl.Buffered(1))


def _out_whole(shape):
    nd = len(shape)
    return pl.BlockSpec(shape, lambda i: (0,) * nd)


_PARAMS = pltpu.CompilerParams(dimension_semantics=("arbitrary",),
                               vmem_limit_bytes=VMEM_LIMIT_BYTES_V7X)


def _mixer_call(x, meta, xs, s_states, w, l, *, B, tT, rows, raw_in):
    R = B * tT
    n_tiles = rows // R
    assert n_tiles * R == rows
    Bs = xs.shape[0]
    body = functools.partial(_mixer_kernel, B=B, tT=tT, n_tiles=n_tiles, Bs=Bs, raw_in=raw_in)
    last = n_tiles - 1
    row_spec = pl.BlockSpec((R, D_MODEL), lambda i: (jnp.minimum(i, last), 0))
    if raw_in:
        assert tT % N_META == 0 and tT > N_META
        spt = tT // N_META
        x_specs = [pl.BlockSpec(
            (pl.Element(B), pl.Element(tT), pl.Element(D_MODEL)),
            lambda i: (0, jnp.maximum(jnp.minimum(i, last) * spt - 1, 0) * N_META, 0)),
            _whole_spec(meta)]
        x_args = [x, meta]
        x_scratch = [pltpu.VMEM((R, D_MODEL), F32)]
    else:
        x_specs, x_args, x_scratch = [row_spec], [x], []
    ssa, ssb, ssh = s_states
    layered = [ssa, ssb, ssh, w["norm_mix"], w["w_in"], w["b_gate"], w["conv_a_w"],
               w["w_a_out"], w["conv_b_w"], w["conv_b_b"], w["w_gates"], w["rg_b_a"],
               w["rg_b_x"], w["rg_lambda"], w["w_b_out"], w["w_o"]]
    st_shapes = lambda b: [(b, CONV_A_TAPS - 1, D_CONV), (b, CONV_B_TAPS - 1, D_RNN), (b, D_RNN)]
    out_shapes = [(rows, D_MODEL)] + st_shapes(B) + [(Bs, D_MODEL)] + st_shapes(Bs)
    ext_a_rows = max(R + (CONV_A_TAPS - 1) * B, CONV_A_TAPS * Bs)
    ext_b_rows = max(R + (CONV_B_TAPS - 1) * B, CONV_B_TAPS * Bs)
    rmax = max(R, Bs)
    return pl.pallas_call(
        body,
        grid=(n_tiles + 1,),
        in_specs=x_specs + [_whole_spec(xs)] + [_layer_spec(a, l) for a in layered],
        out_specs=[row_spec] + [_out_whole(s) for s in out_shapes[1:]],
        out_shape=[jax.ShapeDtypeStruct(s, F32) for s in out_shapes],
        scratch_shapes=[pltpu.VMEM((ext_a_rows, D_CONV), F32),
                        pltpu.VMEM((ext_b_rows, D_RNN), F32),
                        pltpu.VMEM((rmax, D_RNN), F32),
                        pltpu.VMEM((rmax, D_RNN), F32),
                        pltpu.VMEM((max(B, Bs), D_RNN), F32),
                        pltpu.VMEM((rmax, 2 * D_RNN), F32)] + x_scratch,
        compiler_params=_PARAMS,
        name="mixer",
    )(*x_args, xs, *layered)


def _ffn_call(x, xs, ssf, w, l, *, B, tT, rows, final_norm, raw_out, seq):
    R = B * tT
    n_tiles = rows // R
    assert n_tiles * R == rows
    Bs = xs.shape[0]
    body = functools.partial(_ffn_kernel, B=B, tT=tT, n_tiles=n_tiles, Bs=Bs,
                             final_norm=final_norm, raw_out=raw_out)
    last = n_tiles - 1
    row_spec = pl.BlockSpec((R, D_MODEL), lambda i: (jnp.minimum(i, last), 0))
    layered = [ssf, w["norm_ffn"], w["ffn_w_up"], w["ffn_w_gate"], w["ffn_conv_w"],
               w["ffn_conv_b"], w["ffn_w_down"]]
    if raw_out:
        assert n_tiles > 2 and tT % N_META == 0 and tT > N_META
        y_spec = pl.BlockSpec(memory_space=pl.ANY)
        y_shape = (B, seq, D_MODEL)
        y_scratch = [pltpu.VMEM((2, B, tT, D_MODEL), F32), pltpu.SemaphoreType.DMA((1,))]
    else:
        y_spec, y_shape, y_scratch = row_spec, (rows, D_MODEL), []
    out_shapes = [y_shape, (B, CONV_F_TAPS - 1, D_FF), (Bs, D_MODEL), (Bs, CONV_F_TAPS - 1, D_FF)]
    ext_f_rows = max(R + (CONV_F_TAPS - 1) * B, CONV_F_TAPS * Bs)
    return pl.pallas_call(
        body,
        grid=(n_tiles + 1,),
        in_specs=[row_spec, _whole_spec(xs)] + [_layer_spec(a, l) for a in layered]
                 + [_whole_spec(w["norm_final"])],
        out_specs=[y_spec] + [_out_whole(s) for s in out_shapes[1:]],
        out_shape=[jax.ShapeDtypeStruct(s, F32) for s in out_shapes],
        scratch_shapes=[pltpu.VMEM((ext_f_rows, D_FF), F32)] + y_scratch,
        compiler_params=_PARAMS,
        name="ffn",
    )(x, xs, *layered, w["norm_final"])


def _dense_gate_weights(rg_w_a, rg_w_x):
    hpb = MXU_EDGE_V7X // RG_HEAD_DIM
    idx = jnp.arange(MXU_EDGE_V7X) // RG_HEAD_DIM
    on_diag = idx[:, None] == idx[None, :]

    def dense(wt):
        wt = wt.reshape(DEPTH, GATE_BLOCKS, hpb, RG_HEAD_DIM, RG_HEAD_DIM)
        wide = wt.transpose(0, 1, 3, 2, 4).reshape(DEPTH, GATE_BLOCKS, RG_HEAD_DIM, MXU_EDGE_V7X)
        return jnp.where(on_diag, jnp.tile(wide, (1, 1, hpb, 1)), 0.0)

    return jnp.concatenate([dense(rg_w_a), dense(rg_w_x)], axis=-1).astype(BF16)


def kernel(x_prompt, x_sample, state_conv_a, state_conv_b, state_rglru, state_conv_ffn,
           meta_tokens, norm_mix, norm_ffn, norm_final, w_in, b_gate, conv_a_w, w_a_out,
           conv_b_w, conv_b_b, rg_w_a, rg_b_a, rg_w_x, rg_b_x, rg_lambda, w_b_out, w_o,
           ffn_w_up, ffn_w_gate, ffn_conv_w, ffn_conv_b, ffn_w_down):
    assert PAST_LEN > 0
    vec = lambda v: v.reshape(DEPTH, 1, -1)
    w = dict(
        norm_mix=vec(norm_mix), norm_ffn=vec(norm_ffn), norm_final=norm_final.reshape(1, -1),
        w_in=w_in.astype(BF16), b_gate=vec(b_gate), conv_a_w=conv_a_w,
        w_a_out=w_a_out.astype(BF16), conv_b_w=conv_b_w, conv_b_b=vec(conv_b_b),
        w_gates=_dense_gate_weights(rg_w_a, rg_w_x), rg_b_a=vec(rg_b_a), rg_b_x=vec(rg_b_x),
        rg_lambda=vec(rg_lambda), w_b_out=w_b_out.astype(BF16), w_o=w_o.astype(BF16),
        ffn_w_up=ffn_w_up.astype(BF16), ffn_w_gate=ffn_w_gate.astype(BF16),
        ffn_conv_w=ffn_conv_w, ffn_conv_b=vec(ffn_conv_b), ffn_w_down=ffn_w_down.astype(BF16))

    bp, seq, _ = x_prompt.shape
    dt = x_prompt.dtype
    bs, t_s, _ = x_sample.shape
    assert t_s == 1
    rows = (N_META + seq) * bp
    x = x_prompt
    xs = x_sample.reshape(bs, D_MODEL)
    meta = meta_tokens.astype(dt)
    p_a, p_b, p_h, p_f, s_a, s_b, s_h, s_f = ([] for _ in range(8))
    for l in range(DEPTH):
        x, pa, pb, ph, xs, sa, sb, sh = _mixer_call(
            x, meta, xs, (state_conv_a, state_conv_b, state_rglru), w, l,
            B=bp, tT=PROMPT_TILE_STEPS, rows=rows, raw_in=(l == 0))
        last = l == DEPTH - 1
        x, pf, xs, sf = _ffn_call(x, xs, state_conv_ffn, w, l, B=bp,
                                  tT=PROMPT_TILE_STEPS if last else FFN_TILE_STEPS,
                                  rows=rows, final_norm=last, raw_out=last, seq=seq)
        for acc, v in zip((p_a, p_b, p_h, p_f, s_a, s_b, s_h, s_f),
                          (pa, pb, ph, pf, sa, sb, sh, sf)):
            acc.append(v)
    y_sample = xs.reshape(bs, t_s, D_MODEL)
    return (x, y_sample, jnp.stack(p_a), jnp.stack(p_b), jnp.stack(p_h), jnp.stack(p_f),
            jnp.stack(s_a), jnp.stack(s_b), jnp.stack(s_h), jnp.stack(s_f))
```

```python
import functools

import jax
import jax.numpy as jnp
from jax import lax
from jax.experimental import pallas as pl
from jax.experimental.pallas import tpu as pltpu

D_MODEL = 1024
DEPTH = 4
N_META = 16
D_CONV = D_MODEL
D_RNN = D_MODEL
RG_HEADS = 16
RG_HEAD_DIM = D_RNN // RG_HEADS
RG_C = 8.0
D_FF = 2816
EPS = 1e-6
PAST_LEN = 16384
CONV_A_TAPS = 3
CONV_B_TAPS = 4
CONV_F_TAPS = 3

MXU_EDGE_V7X = 256
GATE_BLOCKS = D_RNN // MXU_EDGE_V7X
VMEM_LIMIT_BYTES_V7X = 58 * 1024 * 1024
PROMPT_TILE_STEPS = 48
FFN_TILE_STEPS = 86

F32 = jnp.float32
BF16 = jnp.bfloat16


def _dot(a, b):
    return jnp.dot(a, b, preferred_element_type=F32)


def _rmsnorm(x, g):
    ms = jnp.mean(x * x, axis=-1, keepdims=True)
    return x * lax.rsqrt(ms + EPS) * g


def _sigmoid(z):
    return 0.5 * jnp.tanh(0.5 * z) + 0.5


def _softplus(z):
    return jnp.maximum(z, 0.0) + jnp.log1p(jnp.exp(-jnp.abs(z)))


def _to_time_major(xb):
    b, t, c = xb.shape
    return jnp.swapaxes(xb, 0, 1).reshape(t * b, c)


def _load_state(ext, s_ref, taps, B):
    for k in range(taps - 1):
        ext[k * B:(k + 1) * B, :] = s_ref[:, k, :]


def _store_state(o_ref, tail, taps, B):
    for k in range(taps - 1):
        o_ref[:, k, :] = tail[k * B:(k + 1) * B, :]


def _mixer_tile(x, w, sc, *, B, tT, is_first):
    (nm_ref, w_in_ref, bg_ref, caw_ref, w_a_out_ref, cbw_ref, cbb_ref, wg_ref, bga_ref,
     bgx_ref, lam_ref, w_b_out_ref, w_o_ref) = w
    exta, extb, a_s, b_s, h_s, g_s = sc
    R = B * tT
    na = (CONV_A_TAPS - 1) * B
    nb = (CONV_B_TAPS - 1) * B
    hn = _rmsnorm(x, nm_ref[...]).astype(BF16)

    o_x = 3 * D_CONV
    half_c = (-0.5 * RG_C) * _softplus(-lam_ref[...])
    blk = lambda j: slice(j * MXU_EDGE_V7X, (j + 1) * MXU_EDGE_V7X)
    extb[nb:nb + R, :] = _dot(hn, w_in_ref[:, o_x:o_x + D_RNN])
    wide_a = _dot(hn, w_in_ref[:, 0:3 * D_CONV])
    xc = cbw_ref[0:1, :] * extb[0:R, :]
    for k in range(1, CONV_B_TAPS):
        xc = xc + cbw_ref[k:k + 1, :] * extb[k * B:k * B + R, :]
    xc = xc + cbb_ref[...]
    tail_b = extb[R:R + nb, :]
    extb[0:nb, :] = tail_b
    xcb = xc.astype(BF16)
    for j in range(GATE_BLOCKS):
        g_s[0:R, 2 * j * MXU_EDGE_V7X:2 * (j + 1) * MXU_EDGE_V7X] = _dot(xcb[:, blk(j)], wg_ref[j])
    o_r = 3 * D_CONV + D_RNN
    wide_b = _dot(hn, w_in_ref[:, o_r:o_r + 3 * D_MODEL])

    h_blk = []
    for j in range(GATE_BLOCKS):
        c = blk(j)
        g = g_s[0:R, 2 * j * MXU_EDGE_V7X:2 * (j + 1) * MXU_EDGE_V7X]
        t_r = jnp.tanh(0.5 * (g[:, :MXU_EDGE_V7X] + bga_ref[:, c]))
        log_a = half_c[:, c] * t_r + half_c[:, c]
        i_g = _sigmoid(g[:, MXU_EDGE_V7X:] + bgx_ref[:, c])
        s_neg = jnp.tanh(log_a)
        two_s = -2.0 * s_neg
        mult = jnp.where(s_neg < 0.0, two_s * lax.rsqrt(two_s * (1.0 - s_neg)), 0.0)
        if is_first is not None:
            mult = jnp.where(is_first, 1.0, mult)
        a_s[0:R, c] = jnp.exp(log_a)
        b_s[0:R, c] = mult * (i_g * xc[:, c])

        def step(t, h, c=c):
            r0 = pl.multiple_of(t * B, B)
            h = a_s[pl.ds(r0, B), c] * h + b_s[pl.ds(r0, B), c]
            b_s[pl.ds(r0, B), c] = h
            return h

        h_blk.append(lax.fori_loop(0, tT, step, h_s[0:B, c], unroll=True))
        h_s[0:B, c] = h_blk[j]
    h_last = jnp.concatenate(h_blk, axis=1)

    exta[na:na + R, :] = wide_a[:, D_CONV:2 * D_CONV] * wide_a[:, 2 * D_CONV:3 * D_CONV]
    conv_a = caw_ref[0:1, :] * exta[0:R, :]
    for k in range(1, CONV_A_TAPS):
        conv_a = conv_a + caw_ref[k:k + 1, :] * exta[k * B:k * B + R, :]
    y_a = _dot((wide_a[:, 0:D_CONV] * conv_a).astype(BF16), w_a_out_ref[...])
    tail_a = exta[R:R + na, :]
    exta[0:na, :] = tail_a

    y_b = _dot((jax.nn.gelu(wide_b[:, 0:D_RNN], approximate=True) * b_s[0:R, :]).astype(BF16),
               w_b_out_ref[...])

    mixed = (_sigmoid(wide_b[:, D_RNN:D_RNN + D_MODEL] + bg_ref[:, 0:D_MODEL]) * y_a
             + _sigmoid(wide_b[:, D_RNN + D_MODEL:] + bg_ref[:, D_MODEL:2 * D_MODEL]) * y_b)
    out = x + _dot(mixed.astype(BF16), w_o_ref[...])
    return out, tail_a, tail_b, h_last


def _mixer_kernel(*refs, B, tT, n_tiles, Bs, raw_in):
    if raw_in:
        xraw_ref, meta_ref, *refs = refs
    else:
        x_ref, *refs = refs
    (xs_ref, ssa_ref, ssb_ref, ssh_ref, *refs) = refs
    w, refs = refs[:13], refs[13:]
    (out_ref, oa_ref, ob_ref, oh_ref, ys_ref, soa_ref, sob_ref, soh_ref, *refs) = refs
    sc, rest = refs[:6], refs[6:]
    exta, extb, a_s, b_s, h_s, _ = sc
    R = B * tT
    i = pl.program_id(0)
    na = (CONV_A_TAPS - 1) * B
    nb = (CONV_B_TAPS - 1) * B

    @pl.when(i == 0)
    def _():
        exta[0:na, :] = jnp.zeros((na, D_CONV), F32)
        extb[0:nb, :] = jnp.zeros((nb, D_RNN), F32)
        h_s[0:B, :] = jnp.zeros((B, D_RNN), F32)

    @pl.when(i < n_tiles)
    def _():
        if raw_in:
            (xt,) = rest
            nm_rows = N_META * B

            @pl.when(i == 0)
            def _():
                xt[0:nm_rows, :] = jnp.broadcast_to(
                    meta_ref[...][:, None, :], (N_META, B, D_MODEL)).reshape(nm_rows, D_MODEL)
                xt[nm_rows:R, :] = _to_time_major(xraw_ref[:, 0:tT - N_META, :])

            @pl.when(i > 0)
            def _():
                xt[...] = _to_time_major(xraw_ref[...])

            x = xt[...]
        else:
            x = x_ref[...]
        rows = lax.broadcasted_iota(jnp.int32, (R, MXU_EDGE_V7X), 0)
        is_first = jnp.logical_and(i == 0, rows < B)
        out, tail_a, tail_b, h_last = _mixer_tile(x, w, sc, B=B, tT=tT, is_first=is_first)
        out_ref[...] = out

        @pl.when(i == n_tiles - 1)
        def _():
            _store_state(oa_ref, tail_a, CONV_A_TAPS, B)
            _store_state(ob_ref, tail_b, CONV_B_TAPS, B)
            oh_ref[...] = h_last

    @pl.when(i == n_tiles)
    def _():
        _load_state(exta, ssa_ref, CONV_A_TAPS, Bs)
        _load_state(extb, ssb_ref, CONV_B_TAPS, Bs)
        h_s[0:Bs, :] = ssh_ref[...]
        out, tail_a, tail_b, h_last = _mixer_tile(xs_ref[...], w, sc, B=Bs, tT=1, is_first=None)
        ys_ref[...] = out
        _store_state(soa_ref, tail_a, CONV_A_TAPS, Bs)
        _store_state(sob_ref, tail_b, CONV_B_TAPS, Bs)
        soh_ref[...] = h_last


def _ffn_tile(x, w, extf, *, B, tT, final_norm):
    nf_ref, w_up_ref, w_gate_ref, fcw_ref, fcb_ref, w_down_ref, nfin_ref = w
    R = B * tT
    nf = (CONV_F_TAPS - 1) * B
    hf = _rmsnorm(x, nf_ref[...]).astype(BF16)
    extf[nf:nf + R, :] = _dot(hf, w_up_ref[...])
    uc = fcw_ref[0:1, :] * extf[0:R, :]
    for k in range(1, CONV_F_TAPS):
        uc = uc + fcw_ref[k:k + 1, :] * extf[k * B:k * B + R, :]
    uc = uc + fcb_ref[...]
    tail_f = extf[R:R + nf, :]
    extf[0:nf, :] = tail_f
    gate = _dot(hf, w_gate_ref[...])
    act = (uc * _sigmoid(uc) * gate).astype(BF16)
    y = x + _dot(act, w_down_ref[...])
    if final_norm:
        y = _rmsnorm(y, nfin_ref[...])
    return y, tail_f


def _ffn_kernel(x_ref, xs_ref, ssf_ref, *refs, B, tT, n_tiles, Bs, final_norm, raw_out):
    w, refs = refs[:7], refs[7:]
    out_ref, of_ref, ys_ref, sof_ref, extf, *rest = refs
    i = pl.program_id(0)
    nf = (CONV_F_TAPS - 1) * B

    @pl.when(i == 0)
    def _():
        extf[0:nf, :] = jnp.zeros((nf, D_FF), F32)

    @pl.when(i < n_tiles)
    def _():
        y, tail_f = _ffn_tile(x_ref[...], w, extf, B=B, tT=tT, final_norm=final_norm)
        if raw_out:
            obuf, sem = rest
            slot = lax.rem(i, 2)
            obuf[slot] = jnp.swapaxes(y.reshape(tT, B, D_MODEL), 0, 1)

            def head_copy():
                return pltpu.make_async_copy(obuf.at[0, :, N_META:tT, :],
                                             out_ref.at[:, 0:tT - N_META, :], sem.at[0])

            def tile_copy(step, slot_):
                start = pl.multiple_of(step * tT - N_META, N_META)
                return pltpu.make_async_copy(
                    obuf.at[slot_], out_ref.at[:, pl.ds(start, tT), :], sem.at[0])

            @pl.when(i == 1)
            def _():
                head_copy().wait()

            @pl.when(i > 1)
            def _():
                tile_copy(i - 1, 1 - slot).wait()

            @pl.when(i == 0)
            def _():
                head_copy().start()

            @pl.when(i > 0)
            def _():
                tile_copy(i, slot).start()

            @pl.when(i == n_tiles - 1)
            def _():
                tile_copy(i, slot).wait()
        else:
            out_ref[...] = y

        @pl.when(i == n_tiles - 1)
        def _():
            _store_state(of_ref, tail_f, CONV_F_TAPS, B)

    @pl.when(i == n_tiles)
    def _():
        _load_state(extf, ssf_ref, CONV_F_TAPS, Bs)
        y, tail_f = _ffn_tile(xs_ref[...], w, extf, B=Bs, tT=1, final_norm=final_norm)
        ys_ref[...] = y
        _store_state(sof_ref, tail_f, CONV_F_TAPS, Bs)


def _layer_spec(arr, l):
    nd = arr.ndim - 1
    return pl.BlockSpec((None,) + arr.shape[1:], lambda i: (l,) + (0,) * nd,
                        pipeline_mode=pl.Buffered(1))


def _whole_spec(arr):
    nd = arr.ndim
    return pl.BlockSpec(arr.shape, lambda i: (0,) * nd, pipeline_mode=pl.Buffered(1))


def _out_whole(shape):
    nd = len(shape)
    return pl.BlockSpec(shape, lambda i: (0,) * nd)


_PARAMS = pltpu.CompilerParams(dimension_semantics=("arbitrary",),
                               vmem_limit_bytes=VMEM_LIMIT_BYTES_V7X)


def _mixer_call(x, meta, xs, s_states, w, l, *, B, tT, rows, raw_in):
    R = B * tT
    n_tiles = rows // R
    assert n_tiles * R == rows
    Bs = xs.shape[0]
    body = functools.partial(_mixer_kernel, B=B, tT=tT, n_tiles=n_tiles, Bs=Bs, raw_in=raw_in)
    last = n_tiles - 1
    row_spec = pl.BlockSpec((R, D_MODEL), lambda i: (jnp.minimum(i, last), 0))
    if raw_in:
        assert tT % N_META == 0 and tT > N_META
        spt = tT // N_META
        x_specs = [pl.BlockSpec(
            (pl.Element(B), pl.Element(tT), pl.Element(D_MODEL)),
            lambda i: (0, jnp.maximum(jnp.minimum(i, last) * spt - 1, 0) * N_META, 0)),
            _whole_spec(meta)]
        x_args = [x, meta]
        x_scratch = [pltpu.VMEM((R, D_MODEL), F32)]
    else:
        x_specs, x_args, x_scratch = [row_spec], [x], []
    ssa, ssb, ssh = s_states
    layered = [ssa, ssb, ssh, w["norm_mix"], w["w_in"], w["b_gate"], w["conv_a_w"],
               w["w_a_out"], w["conv_b_w"], w["conv_b_b"], w["w_gates"], w["rg_b_a"],
               w["rg_b_x"], w["rg_lambda"], w["w_b_out"], w["w_o"]]
    st_shapes = lambda b: [(b, CONV_A_TAPS - 1, D_CONV), (b, CONV_B_TAPS - 1, D_RNN), (b, D_RNN)]
    out_shapes = [(rows, D_MODEL)] + st_shapes(B) + [(Bs, D_MODEL)] + st_shapes(Bs)
    ext_a_rows = max(R + (CONV_A_TAPS - 1) * B, CONV_A_TAPS * Bs)
    ext_b_rows = max(R + (CONV_B_TAPS - 1) * B, CONV_B_TAPS * Bs)
    rmax = max(R, Bs)
    return pl.pallas_call(
        body,
        grid=(n_tiles + 1,),
        in_specs=x_specs + [_whole_spec(xs)] + [_layer_spec(a, l) for a in layered],
        out_specs=[row_spec] + [_out_whole(s) for s in out_shapes[1:]],
        out_shape=[jax.ShapeDtypeStruct(s, F32) for s in out_shapes],
        scratch_shapes=[pltpu.VMEM((ext_a_rows, D_CONV), F32),
                        pltpu.VMEM((ext_b_rows, D_RNN), F32),
                        pltpu.VMEM((rmax, D_RNN), F32),
                        pltpu.VMEM((rmax, D_RNN), F32),
                        pltpu.VMEM((max(B, Bs), D_RNN), F32),
                        pltpu.VMEM((rmax, 2 * D_RNN), F32)] + x_scratch,
        compiler_params=_PARAMS,
        name="mixer",
    )(*x_args, xs, *layered)


def _ffn_call(x, xs, ssf, w, l, *, B, tT, rows, final_norm, raw_out, seq):
    R = B * tT
    n_tiles = rows // R
    assert n_tiles * R == rows
    Bs = xs.shape[0]
    body = functools.partial(_ffn_kernel, B=B, tT=tT, n_tiles=n_tiles, Bs=Bs,
                             final_norm=final_norm, raw_out=raw_out)
    last = n_tiles - 1
    row_spec = pl.BlockSpec((R, D_MODEL), lambda i: (jnp.minimum(i, last), 0))
    layered = [ssf, w["norm_ffn"], w["ffn_w_up"], w["ffn_w_gate"], w["ffn_conv_w"],
               w["ffn_conv_b"], w["ffn_w_down"]]
    if raw_out:
        assert n_tiles > 2 and tT % N_META == 0 and tT > N_META
        y_spec = pl.BlockSpec(memory_space=pl.ANY)
        y_shape = (B, seq, D_MODEL)
        y_scratch = [pltpu.VMEM((2, B, tT, D_MODEL), F32), pltpu.SemaphoreType.DMA((1,))]
    else:
        y_spec, y_shape, y_scratch = row_spec, (rows, D_MODEL), []
    out_shapes = [y_shape, (B, CONV_F_TAPS - 1, D_FF), (Bs, D_MODEL), (Bs, CONV_F_TAPS - 1, D_FF)]
    ext_f_rows = max(R + (CONV_F_TAPS - 1) * B, CONV_F_TAPS * Bs)
    return pl.pallas_call(
        body,
        grid=(n_tiles + 1,),
        in_specs=[row_spec, _whole_spec(xs)] + [_layer_spec(a, l) for a in layered]
                 + [_whole_spec(w["norm_final"])],
        out_specs=[y_spec] + [_out_whole(s) for s in out_shapes[1:]],
        out_shape=[jax.ShapeDtypeStruct(s, F32) for s in out_shapes],
        scratch_shapes=[pltpu.VMEM((ext_f_rows, D_FF), F32)] + y_scratch,
        compiler_params=_PARAMS,
        name="ffn",
    )(x, xs, *layered, w["norm_final"])


def _dense_gate_weights(rg_w_a, rg_w_x):
    hpb = MXU_EDGE_V7X // RG_HEAD_DIM
    idx = jnp.arange(MXU_EDGE_V7X) // RG_HEAD_DIM
    on_diag = idx[:, None] == idx[None, :]

    def dense(wt):
        wt = wt.reshape(DEPTH, GATE_BLOCKS, hpb, RG_HEAD_DIM, RG_HEAD_DIM)
        wide = wt.transpose(0, 1, 3, 2, 4).reshape(DEPTH, GATE_BLOCKS, RG_HEAD_DIM, MXU_EDGE_V7X)
        return jnp.where(on_diag, jnp.tile(wide, (1, 1, hpb, 1)), 0.0)

    return jnp.concatenate([dense(rg_w_a), dense(rg_w_x)], axis=-1).astype(BF16)


def kernel(x_prompt, x_sample, state_conv_a, state_conv_b, state_rglru, state_conv_ffn,
           meta_tokens, norm_mix, norm_ffn, norm_final, w_in, b_gate, conv_a_w, w_a_out,
           conv_b_w, conv_b_b, rg_w_a, rg_b_a, rg_w_x, rg_b_x, rg_lambda, w_b_out, w_o,
           ffn_w_up, ffn_w_gate, ffn_conv_w, ffn_conv_b, ffn_w_down):
    assert PAST_LEN > 0
    vec = lambda v: v.reshape(DEPTH, 1, -1)
    w = dict(
        norm_mix=vec(norm_mix), norm_ffn=vec(norm_ffn), norm_final=norm_final.reshape(1, -1),
        w_in=w_in.astype(BF16), b_gate=vec(b_gate), conv_a_w=conv_a_w,
        w_a_out=w_a_out.astype(BF16), conv_b_w=conv_b_w, conv_b_b=vec(conv_b_b),
        w_gates=_dense_gate_weights(rg_w_a, rg_w_x), rg_b_a=vec(rg_b_a), rg_b_x=vec(rg_b_x),
        rg_lambda=vec(rg_lambda), w_b_out=w_b_out.astype(BF16), w_o=w_o.astype(BF16),
        ffn_w_up=ffn_w_up.astype(BF16), ffn_w_gate=ffn_w_gate.astype(BF16),
        ffn_conv_w=ffn_conv_w, ffn_conv_b=vec(ffn_conv_b), ffn_w_down=ffn_w_down.astype(BF16))

    bp, seq, _ = x_prompt.shape
    dt = x_prompt.dtype
    bs, t_s, _ = x_sample.shape
    assert t_s == 1
    rows = (N_META + seq) * bp
    x = x_prompt
    xs = x_sample.reshape(bs, D_MODEL)
    meta = meta_tokens.astype(dt)
    p_a, p_b, p_h, p_f, s_a, s_b, s_h, s_f = ([] for _ in range(8))
    for l in range(DEPTH):
        x, pa, pb, ph, xs, sa, sb, sh = _mixer_call(
            x, meta, xs, (state_conv_a, state_conv_b, state_rglru), w, l,
            B=bp, tT=PROMPT_TILE_STEPS, rows=rows, raw_in=(l == 0))
        last = l == DEPTH - 1
        x, pf, xs, sf = _ffn_call(x, xs, state_conv_ffn, w, l, B=bp,
                                  tT=PROMPT_TILE_STEPS if last else FFN_TILE_STEPS,
                                  rows=rows, final_norm=last, raw_out=last, seq=seq)
        for acc, v in zip((p_a, p_b, p_h, p_f, s_a, s_b, s_h, s_f),
                          (pa, pb, ph, pf, sa, sb, sh, sf)):
            acc.append(v)
    y_sample = xs.reshape(bs, t_s, D_MODEL)
    return (x, y_sample, jnp.stack(p_a), jnp.stack(p_b), jnp.stack(p_h), jnp.stack(p_f),
            jnp.stack(s_a), jnp.stack(s_b), jnp.stack(s_h), jnp.stack(s_f))
```

```python
import functools

import jax
import jax.numpy as jnp
from jax import lax
from jax.experimental import pallas as pl
from jax.experimental.pallas import tpu as pltpu

D_MODEL = 1024
DEPTH = 4
N_META = 16
D_CONV = D_MODEL
D_RNN = D_MODEL
RG_HEADS = 16
RG_HEAD_DIM = D_RNN // RG_HEADS
RG_C = 8.0
D_FF = 2816
EPS = 1e-6
PAST_LEN = 16384
CONV_A_TAPS = 3
CONV_B_TAPS = 4
CONV_F_TAPS = 3

MXU_EDGE_V7X = 256
GATE_BLOCKS = D_RNN // MXU_EDGE_V7X
VMEM_LIMIT_BYTES_V7X = 58 * 1024 * 1024
PROMPT_TILE_STEPS = 48
FFN_TILE_STEPS = 86

F32 = jnp.float32
BF16 = jnp.bfloat16


def _dot(a, b):
    return jnp.dot(a, b, preferred_element_type=F32)


def _rmsnorm(x, g):
    ms = jnp.mean(x * x, axis=-1, keepdims=True)
    return x * lax.rsqrt(ms + EPS) * g


def _sigmoid(z):
    return 0.5 * jnp.tanh(0.5 * z) + 0.5


def _softplus(z):
    return jnp.maximum(z, 0.0) + jnp.log1p(jnp.exp(-jnp.abs(z)))


def _to_time_major(xb):
    b, t, c = xb.shape
    return jnp.swapaxes(xb, 0, 1).reshape(t * b, c)


def _load_state(ext, s_ref, taps, B):
    for k in range(taps - 1):
        ext[k * B:(k + 1) * B, :] = s_ref[:, k, :]


def _store_state(o_ref, tail, taps, B):
    for k in range(taps - 1):
        o_ref[:, k, :] = tail[k * B:(k + 1) * B, :]


def _mixer_tile(x, w, sc, *, B, tT, is_first):
    (nm_ref, w_in_ref, bg_ref, caw_ref, w_a_out_ref, cbw_ref, cbb_ref, wg_ref, bga_ref,
     bgx_ref, lam_ref, w_b_out_ref, w_o_ref) = w
    exta, extb, a_s, b_s, h_s, g_s = sc
    R = B * tT
    na = (CONV_A_TAPS - 1) * B
    nb = (CONV_B_TAPS - 1) * B
    hn = _rmsnorm(x, nm_ref[...]).astype(BF16)

    o_x = 3 * D_CONV
    half_c = (0.5 * RG_C) * _softplus(-lam_ref[...])
    half_ba = 0.5 * bga_ref[...]
    half_bx = 0.5 * bgx_ref[...]
    blk = lambda j: slice(j * MXU_EDGE_V7X, (j + 1) * MXU_EDGE_V7X)
    extb[nb:nb + R, :] = _dot(hn, w_in_ref[:, o_x:o_x + D_RNN])
    wide_a = _dot(hn, w_in_ref[:, 0:3 * D_CONV])
    xc = cbw_ref[0:1, :] * extb[0:R, :]
    for k in range(1, CONV_B_TAPS):
        xc = xc + cbw_ref[k:k + 1, :] * extb[k * B:k * B + R, :]
    xc = xc + cbb_ref[...]
    tail_b = extb[R:R + nb, :]
    extb[0:nb, :] = tail_b
    xcb = xc.astype(BF16)
    for j in range(GATE_BLOCKS):
        g_s[0:R, 2 * j * MXU_EDGE_V7X:2 * (j + 1) * MXU_EDGE_V7X] = _dot(xcb[:, blk(j)], wg_ref[j])
    o_r = 3 * D_CONV + D_RNN
    wide_b = _dot(hn, w_in_ref[:, o_r:o_r + 3 * D_MODEL])

    h_blk = []
    for j in range(GATE_BLOCKS):
        c = blk(j)
        g = g_s[0:R, 2 * j * MXU_EDGE_V7X:2 * (j + 1) * MXU_EDGE_V7X]
        t_r = jnp.tanh(g[:, :MXU_EDGE_V7X] + half_ba[:, c])
        neg_log_a = half_c[:, c] * t_r + half_c[:, c]
        t_x = jnp.tanh(g[:, MXU_EDGE_V7X:] + half_bx[:, c])
        s = jnp.tanh(neg_log_a)
        half_mult = jnp.where(s > 0.0, s * lax.rsqrt((s + s) * (s + 1.0)), 0.0)
        if is_first is not None:
            half_mult = jnp.where(is_first, 0.5, half_mult)
        a_s[0:R, c] = jnp.exp(-neg_log_a)
        q = half_mult * xc[:, c]
        b_s[0:R, c] = q * t_x + q

        def step(t, h, c=c):
            r0 = pl.multiple_of(t * B, B)
            h = a_s[pl.ds(r0, B), c] * h + b_s[pl.ds(r0, B), c]
            b_s[pl.ds(r0, B), c] = h
            return h

        h_blk.append(lax.fori_loop(0, tT, step, h_s[0:B, c], unroll=True))
        h_s[0:B, c] = h_blk[j]
    h_last = jnp.concatenate(h_blk, axis=1)

    exta[na:na + R, :] = wide_a[:, D_CONV:2 * D_CONV] * wide_a[:, 2 * D_CONV:3 * D_CONV]
    conv_a = caw_ref[0:1, :] * exta[0:R, :]
    for k in range(1, CONV_A_TAPS):
        conv_a = conv_a + caw_ref[k:k + 1, :] * exta[k * B:k * B + R, :]
    y_a = _dot((wide_a[:, 0:D_CONV] * conv_a).astype(BF16), w_a_out_ref[...])
    tail_a = exta[R:R + na, :]
    exta[0:na, :] = tail_a

    y_b = _dot((jax.nn.gelu(wide_b[:, 0:D_RNN], approximate=True) * b_s[0:R, :]).astype(BF16),
               w_b_out_ref[...])

    mixed = (_sigmoid(wide_b[:, D_RNN:D_RNN + D_MODEL] + bg_ref[:, 0:D_MODEL]) * y_a
             + _sigmoid(wide_b[:, D_RNN + D_MODEL:] + bg_ref[:, D_MODEL:2 * D_MODEL]) * y_b)
    out = x + _dot(mixed.astype(BF16), w_o_ref[...])
    return out, tail_a, tail_b, h_last


def _mixer_kernel(*refs, B, tT, n_tiles, Bs, raw_in):
    if raw_in:
        xraw_ref, meta_ref, *refs = refs
    else:
        x_ref, *refs = refs
    (xs_ref, ssa_ref, ssb_ref, ssh_ref, *refs) = refs
    w, refs = refs[:13], refs[13:]
    (out_ref, oa_ref, ob_ref, oh_ref, ys_ref, soa_ref, sob_ref, soh_ref, *refs) = refs
    sc, rest = refs[:6], refs[6:]
    exta, extb, a_s, b_s, h_s, _ = sc
    R = B * tT
    i = pl.program_id(0)
    na = (CONV_A_TAPS - 1) * B
    nb = (CONV_B_TAPS - 1) * B

    @pl.when(i == 0)
    def _():
        exta[0:na, :] = jnp.zeros((na, D_CONV), F32)
        extb[0:nb, :] = jnp.zeros((nb, D_RNN), F32)
        h_s[0:B, :] = jnp.zeros((B, D_RNN), F32)

    @pl.when(i < n_tiles)
    def _():
        if raw_in:
            (xt,) = rest
            nm_rows = N_META * B

            @pl.when(i == 0)
            def _():
                xt[0:nm_rows, :] = jnp.broadcast_to(
                    meta_ref[...][:, None, :], (N_META, B, D_MODEL)).reshape(nm_rows, D_MODEL)
                xt[nm_rows:R, :] = _to_time_major(xraw_ref[:, 0:tT - N_META, :])

            @pl.when(i > 0)
            def _():
                xt[...] = _to_time_major(xraw_ref[...])

            x = xt[...]
        else:
            x = x_ref[...]
        rows = lax.broadcasted_iota(jnp.int32, (R, MXU_EDGE_V7X), 0)
        is_first = jnp.logical_and(i == 0, rows < B)
        out, tail_a, tail_b, h_last = _mixer_tile(x, w, sc, B=B, tT=tT, is_first=is_first)
        out_ref[...] = out

        @pl.when(i == n_tiles - 1)
        def _():
            _store_state(oa_ref, tail_a, CONV_A_TAPS, B)
            _store_state(ob_ref, tail_b, CONV_B_TAPS, B)
            oh_ref[...] = h_last

    @pl.when(i == n_tiles)
    def _():
        _load_state(exta, ssa_ref, CONV_A_TAPS, Bs)
        _load_state(extb, ssb_ref, CONV_B_TAPS, Bs)
        h_s[0:Bs, :] = ssh_ref[...]
        out, tail_a, tail_b, h_last = _mixer_tile(xs_ref[...], w, sc, B=Bs, tT=1, is_first=None)
        ys_ref[...] = out
        _store_state(soa_ref, tail_a, CONV_A_TAPS, Bs)
        _store_state(sob_ref, tail_b, CONV_B_TAPS, Bs)
        soh_ref[...] = h_last


def _ffn_tile(x, w, extf, *, B, tT, final_norm):
    nf_ref, w_up_ref, w_gate_ref, fcw_ref, fcb_ref, w_down_ref, nfin_ref = w
    R = B * tT
    nf = (CONV_F_TAPS - 1) * B
    hf = _rmsnorm(x, nf_ref[...]).astype(BF16)
    extf[nf:nf + R, :] = _dot(hf, w_up_ref[...])
    uc = fcw_ref[0:1, :] * extf[0:R, :]
    for k in range(1, CONV_F_TAPS):
        uc = uc + fcw_ref[k:k + 1, :] * extf[k * B:k * B + R, :]
    uc = uc + fcb_ref[...]
    tail_f = extf[R:R + nf, :]
    extf[0:nf, :] = tail_f
    gate = _dot(hf, w_gate_ref[...])
    act = (uc * _sigmoid(uc) * gate).astype(BF16)
    y = x + _dot(act, w_down_ref[...])
    if final_norm:
        y = _rmsnorm(y, nfin_ref[...])
    return y, tail_f


def _ffn_kernel(x_ref, xs_ref, ssf_ref, *refs, B, tT, n_tiles, Bs, final_norm, raw_out):
    w, refs = refs[:7], refs[7:]
    out_ref, of_ref, ys_ref, sof_ref, extf, *rest = refs
    i = pl.program_id(0)
    nf = (CONV_F_TAPS - 1) * B

    @pl.when(i == 0)
    def _():
        extf[0:nf, :] = jnp.zeros((nf, D_FF), F32)

    @pl.when(i < n_tiles)
    def _():
        y, tail_f = _ffn_tile(x_ref[...], w, extf, B=B, tT=tT, final_norm=final_norm)
        if raw_out:
            obuf, sem = rest
            slot = lax.rem(i, 2)
            obuf[slot] = jnp.swapaxes(y.reshape(tT, B, D_MODEL), 0, 1)

            def head_copy():
                return pltpu.make_async_copy(obuf.at[0, :, N_META:tT, :],
                                             out_ref.at[:, 0:tT - N_META, :], sem.at[0])

            def tile_copy(step, slot_):
                start = pl.multiple_of(step * tT - N_META, N_META)
                return pltpu.make_async_copy(
                    obuf.at[slot_], out_ref.at[:, pl.ds(start, tT), :], sem.at[0])

            @pl.when(i == 1)
            def _():
                head_copy().wait()

            @pl.when(i > 1)
            def _():
                tile_copy(i - 1, 1 - slot).wait()

            @pl.when(i == 0)
            def _():
                head_copy().start()

            @pl.when(i > 0)
            def _():
                tile_copy(i, slot).start()

            @pl.when(i == n_tiles - 1)
            def _():
                tile_copy(i, slot).wait()
        else:
            out_ref[...] = y

        @pl.when(i == n_tiles - 1)
        def _():
            _store_state(of_ref, tail_f, CONV_F_TAPS, B)

    @pl.when(i == n_tiles)
    def _():
        _load_state(extf, ssf_ref, CONV_F_TAPS, Bs)
        y, tail_f = _ffn_tile(xs_ref[...], w, extf, B=Bs, tT=1, final_norm=final_norm)
        ys_ref[...] = y
        _store_state(sof_ref, tail_f, CONV_F_TAPS, Bs)


def _layer_spec(arr, l):
    nd = arr.ndim - 1
    return pl.BlockSpec((None,) + arr.shape[1:], lambda i: (l,) + (0,) * nd,
                        pipeline_mode=pl.Buffered(1))


def _whole_spec(arr):
    nd = arr.ndim
    return pl.BlockSpec(arr.shape, lambda i: (0,) * nd, pipeline_mode=pl.Buffered(1))


def _out_whole(shape):
    nd = len(shape)
    return pl.BlockSpec(shape, lambda i: (0,) * nd)


_PARAMS = pltpu.CompilerParams(dimension_semantics=("arbitrary",),
                               vmem_limit_bytes=VMEM_LIMIT_BYTES_V7X)


def _mixer_call(x, meta, xs, s_states, w, l, *, B, tT, rows, raw_in):
    R = B * tT
    n_tiles = rows // R
    assert n_tiles * R == rows
    Bs = xs.shape[0]
    body = functools.partial(_mixer_kernel, B=B, tT=tT, n_tiles=n_tiles, Bs=Bs, raw_in=raw_in)
    last = n_tiles - 1
    row_spec = pl.BlockSpec((R, D_MODEL), lambda i: (jnp.minimum(i, last), 0))
    if raw_in:
        assert tT % N_META == 0 and tT > N_META
        spt = tT // N_META
        x_specs = [pl.BlockSpec(
            (pl.Element(B), pl.Element(tT), pl.Element(D_MODEL)),
            lambda i: (0, jnp.maximum(jnp.minimum(i, last) * spt - 1, 0) * N_META, 0)),
            _whole_spec(meta)]
        x_args = [x, meta]
        x_scratch = [pltpu.VMEM((R, D_MODEL), F32)]
    else:
        x_specs, x_args, x_scratch = [row_spec], [x], []
    ssa, ssb, ssh = s_states
    layered = [ssa, ssb, ssh, w["norm_mix"], w["w_in"], w["b_gate"], w["conv_a_w"],
               w["w_a_out"], w["conv_b_w"], w["conv_b_b"], w["w_gates"], w["rg_b_a"],
               w["rg_b_x"], w["rg_lambda"], w["w_b_out"], w["w_o"]]
    st_shapes = lambda b: [(b, CONV_A_TAPS - 1, D_CONV), (b, CONV_B_TAPS - 1, D_RNN), (b, D_RNN)]
    out_shapes = [(rows, D_MODEL)] + st_shapes(B) + [(Bs, D_MODEL)] + st_shapes(Bs)
    ext_a_rows = max(R + (CONV_A_TAPS - 1) * B, CONV_A_TAPS * Bs)
    ext_b_rows = max(R + (CONV_B_TAPS - 1) * B, CONV_B_TAPS * Bs)
    rmax = max(R, Bs)
    return pl.pallas_call(
        body,
        grid=(n_tiles + 1,),
        in_specs=x_specs + [_whole_spec(xs)] + [_layer_spec(a, l) for a in layered],
        out_specs=[row_spec] + [_out_whole(s) for s in out_shapes[1:]],
        out_shape=[jax.ShapeDtypeStruct(s, F32) for s in out_shapes],
        scratch_shapes=[pltpu.VMEM((ext_a_rows, D_CONV), F32),
                        pltpu.VMEM((ext_b_rows, D_RNN), F32),
                        pltpu.VMEM((rmax, D_RNN), F32),
                        pltpu.VMEM((rmax, D_RNN), F32),
                        pltpu.VMEM((max(B, Bs), D_RNN), F32),
                        pltpu.VMEM((rmax, 2 * D_RNN), F32)] + x_scratch,
        compiler_params=_PARAMS,
        name="mixer",
    )(*x_args, xs, *layered)


def _ffn_call(x, xs, ssf, w, l, *, B, tT, rows, final_norm, raw_out, seq):
    R = B * tT
    n_tiles = rows // R
    assert n_tiles * R == rows
    Bs = xs.shape[0]
    body = functools.partial(_ffn_kernel, B=B, tT=tT, n_tiles=n_tiles, Bs=Bs,
                             final_norm=final_norm, raw_out=raw_out)
    last = n_tiles - 1
    row_spec = pl.BlockSpec((R, D_MODEL), lambda i: (jnp.minimum(i, last), 0))
    layered = [ssf, w["norm_ffn"], w["ffn_w_up"], w["ffn_w_gate"], w["ffn_conv_w"],
               w["ffn_conv_b"], w["ffn_w_down"]]
    if raw_out:
        assert n_tiles > 2 and tT % N_META == 0 and tT > N_META
        y_spec = pl.BlockSpec(memory_space=pl.ANY)
        y_shape = (B, seq, D_MODEL)
        y_scratch = [pltpu.VMEM((2, B, tT, D_MODEL), F32), pltpu.SemaphoreType.DMA((1,))]
    else:
        y_spec, y_shape, y_scratch = row_spec, (rows, D_MODEL), []
    out_shapes = [y_shape, (B, CONV_F_TAPS - 1, D_FF), (Bs, D_MODEL), (Bs, CONV_F_TAPS - 1, D_FF)]
    ext_f_rows = max(R + (CONV_F_TAPS - 1) * B, CONV_F_TAPS * Bs)
    return pl.pallas_call(
        body,
        grid=(n_tiles + 1,),
        in_specs=[row_spec, _whole_spec(xs)] + [_layer_spec(a, l) for a in layered]
                 + [_whole_spec(w["norm_final"])],
        out_specs=[y_spec] + [_out_whole(s) for s in out_shapes[1:]],
        out_shape=[jax.ShapeDtypeStruct(s, F32) for s in out_shapes],
        scratch_shapes=[pltpu.VMEM((ext_f_rows, D_FF), F32)] + y_scratch,
        compiler_params=_PARAMS,
        name="ffn",
    )(x, xs, *layered, w["norm_final"])


def _dense_gate_weights(rg_w_a, rg_w_x):
    hpb = MXU_EDGE_V7X // RG_HEAD_DIM
    idx = jnp.arange(MXU_EDGE_V7X) // RG_HEAD_DIM
    on_diag = idx[:, None] == idx[None, :]

    def dense(wt):
        wt = wt.reshape(DEPTH, GATE_BLOCKS, hpb, RG_HEAD_DIM, RG_HEAD_DIM)
        wide = wt.transpose(0, 1, 3, 2, 4).reshape(DEPTH, GATE_BLOCKS, RG_HEAD_DIM, MXU_EDGE_V7X)
        return jnp.where(on_diag, 0.5 * jnp.tile(wide, (1, 1, hpb, 1)), 0.0)

    return jnp.concatenate([dense(rg_w_a), dense(rg_w_x)], axis=-1).astype(BF16)


def kernel(x_prompt, x_sample, state_conv_a, state_conv_b, state_rglru, state_conv_ffn,
           meta_tokens, norm_mix, norm_ffn, norm_final, w_in, b_gate, conv_a_w, w_a_out,
           conv_b_w, conv_b_b, rg_w_a, rg_b_a, rg_w_x, rg_b_x, rg_lambda, w_b_out, w_o,
           ffn_w_up, ffn_w_gate, ffn_conv_w, ffn_conv_b, ffn_w_down):
    assert PAST_LEN > 0
    vec = lambda v: v.reshape(DEPTH, 1, -1)
    w = dict(
        norm_mix=vec(norm_mix), norm_ffn=vec(norm_ffn), norm_final=norm_final.reshape(1, -1),
        w_in=w_in.astype(BF16), b_gate=vec(b_gate), conv_a_w=conv_a_w,
        w_a_out=w_a_out.astype(BF16), conv_b_w=conv_b_w, conv_b_b=vec(conv_b_b),
        w_gates=_dense_gate_weights(rg_w_a, rg_w_x), rg_b_a=vec(rg_b_a), rg_b_x=vec(rg_b_x),
        rg_lambda=vec(rg_lambda), w_b_out=w_b_out.astype(BF16), w_o=w_o.astype(BF16),
        ffn_w_up=ffn_w_up.astype(BF16), ffn_w_gate=ffn_w_gate.astype(BF16),
        ffn_conv_w=ffn_conv_w, ffn_conv_b=vec(ffn_conv_b), ffn_w_down=ffn_w_down.astype(BF16))

    bp, seq, _ = x_prompt.shape
    dt = x_prompt.dtype
    bs, t_s, _ = x_sample.shape
    assert t_s == 1
    rows = (N_META + seq) * bp
    x = x_prompt
    xs = x_sample.reshape(bs, D_MODEL)
    meta = meta_tokens.astype(dt)
    p_a, p_b, p_h, p_f, s_a, s_b, s_h, s_f = ([] for _ in range(8))
    for l in range(DEPTH):
        x, pa, pb, ph, xs, sa, sb, sh = _mixer_call(
            x, meta, xs, (state_conv_a, state_conv_b, state_rglru), w, l,
            B=bp, tT=PROMPT_TILE_STEPS, rows=rows, raw_in=(l == 0))
        last = l == DEPTH - 1
        x, pf, xs, sf = _ffn_call(x, xs, state_conv_ffn, w, l, B=bp,
                                  tT=PROMPT_TILE_STEPS if last else FFN_TILE_STEPS,
                                  rows=rows, final_norm=last, raw_out=last, seq=seq)
        for acc, v in zip((p_a, p_b, p_h, p_f, s_a, s_b, s_h, s_f),
                          (pa, pb, ph, pf, sa, sb, sh, sf)):
            acc.append(v)
    y_sample = xs.reshape(bs, t_s, D_MODEL)
    return (x, y_sample, jnp.stack(p_a), jnp.stack(p_b), jnp.stack(p_h), jnp.stack(p_f),
            jnp.stack(s_a), jnp.stack(s_b), jnp.stack(s_h), jnp.stack(s_f))
```

```python
import functools

import jax
import jax.numpy as jnp
from jax import lax
from jax.experimental import pallas as pl
from jax.experimental.pallas import tpu as pltpu

D_MODEL = 1024
DEPTH = 4
N_META = 16
D_CONV = D_MODEL
D_RNN = D_MODEL
RG_HEADS = 16
RG_HEAD_DIM = D_RNN // RG_HEADS
RG_C = 8.0
D_FF = 2816
EPS = 1e-6
PAST_LEN = 16384
CONV_A_TAPS = 3
CONV_B_TAPS = 4
CONV_F_TAPS = 3

MXU_EDGE_V7X = 256
GATE_BLOCKS = D_RNN // MXU_EDGE_V7X
VMEM_LIMIT_BYTES_V7X = 58 * 1024 * 1024
PROMPT_TILE_STEPS = 48
FFN_TILE_STEPS = 86

F32 = jnp.float32
BF16 = jnp.bfloat16


def _dot(a, b):
    return jnp.dot(a, b, preferred_element_type=F32)


def _rmsnorm(x, g):
    ms = jnp.mean(x * x, axis=-1, keepdims=True)
    return x * lax.rsqrt(ms + EPS) * g


def _sigmoid(z):
    return 0.5 * jnp.tanh(0.5 * z) + 0.5


def _softplus(z):
    return jnp.maximum(z, 0.0) + jnp.log1p(jnp.exp(-jnp.abs(z)))


def _to_time_major(xb):
    b, t, c = xb.shape
    return jnp.swapaxes(xb, 0, 1).reshape(t * b, c)


def _load_state(ext, s_ref, taps, B):
    for k in range(taps - 1):
        ext[k * B:(k + 1) * B, :] = s_ref[:, k, :]


def _store_state(o_ref, tail, taps, B):
    for k in range(taps - 1):
        o_ref[:, k, :] = tail[k * B:(k + 1) * B, :]


def _mixer_tile(x, w, sc, *, B, tT, is_first):
    (nm_ref, w_in_ref, bg_ref, caw_ref, w_a_out_ref, cbw_ref, cbb_ref, wg_ref, bga_ref,
     bgx_ref, lam_ref, w_b_out_ref, w_o_ref) = w
    exta, extb, b_s, h_s, g_s = sc
    R = B * tT
    na = (CONV_A_TAPS - 1) * B
    nb = (CONV_B_TAPS - 1) * B
    hn = _rmsnorm(x, nm_ref[...]).astype(BF16)

    o_x = 3 * D_CONV
    half_c = (0.5 * RG_C) * _softplus(-lam_ref[...])
    half_ba = 0.5 * bga_ref[...]
    half_bx = 0.5 * bgx_ref[...]
    blk = lambda j: slice(j * MXU_EDGE_V7X, (j + 1) * MXU_EDGE_V7X)
    extb[nb:nb + R, :] = _dot(hn, w_in_ref[:, o_x:o_x + D_RNN])
    wide_a = _dot(hn, w_in_ref[:, 0:3 * D_CONV])
    xc = cbw_ref[0:1, :] * extb[0:R, :]
    for k in range(1, CONV_B_TAPS):
        xc = xc + cbw_ref[k:k + 1, :] * extb[k * B:k * B + R, :]
    xc = xc + cbb_ref[...]
    tail_b = extb[R:R + nb, :]
    extb[0:nb, :] = tail_b
    xcb = xc.astype(BF16)
    for j in range(GATE_BLOCKS):
        g = _dot(xcb[:, blk(j)], wg_ref[j])
        g_s[0:R, blk(j)] = g[:, :MXU_EDGE_V7X]
        g_s[0:R, D_RNN + j * MXU_EDGE_V7X:D_RNN + (j + 1) * MXU_EDGE_V7X] = g[:, MXU_EDGE_V7X:]
    o_r = 3 * D_CONV + D_RNN
    wide_b = _dot(hn, w_in_ref[:, o_r:o_r + 3 * D_MODEL])

    steps = min(tT, 8)
    assert tT % steps == 0
    rc = steps * B
    h = h_s[0:B, :]
    for k in range(tT // steps):
        r = slice(k * rc, (k + 1) * rc)
        t_r = jnp.tanh(g_s[r, 0:D_RNN] + half_ba)
        neg_log_a = half_c * t_r + half_c
        t_x = jnp.tanh(g_s[r, D_RNN:2 * D_RNN] + half_bx)
        s = jnp.tanh(neg_log_a)
        half_mult = jnp.where(s > 0.0, s * lax.rsqrt((s + s) * (s + 1.0)), 0.0)
        if is_first is not None and k == 0:
            half_mult = jnp.concatenate(
                [jnp.where(is_first, 0.5, half_mult[0:B]), half_mult[B:]], axis=0)
        a = jnp.exp(-neg_log_a)
        q = half_mult * xc[r, :]
        b = q * t_x + q
        for t in range(steps):
            h = a[t * B:(t + 1) * B] * h + b[t * B:(t + 1) * B]
            b_s[k * rc + t * B:k * rc + (t + 1) * B, :] = h
    h_s[0:B, :] = h
    h_last = h

    exta[na:na + R, :] = wide_a[:, D_CONV:2 * D_CONV] * wide_a[:, 2 * D_CONV:3 * D_CONV]
    conv_a = caw_ref[0:1, :] * exta[0:R, :]
    for k in range(1, CONV_A_TAPS):
        conv_a = conv_a + caw_ref[k:k + 1, :] * exta[k * B:k * B + R, :]
    y_a = _dot((wide_a[:, 0:D_CONV] * conv_a).astype(BF16), w_a_out_ref[...])
    tail_a = exta[R:R + na, :]
    exta[0:na, :] = tail_a

    y_b = _dot((jax.nn.gelu(wide_b[:, 0:D_RNN], approximate=True) * b_s[0:R, :]).astype(BF16),
               w_b_out_ref[...])

    mixed = (_sigmoid(wide_b[:, D_RNN:D_RNN + D_MODEL] + bg_ref[:, 0:D_MODEL]) * y_a
             + _sigmoid(wide_b[:, D_RNN + D_MODEL:] + bg_ref[:, D_MODEL:2 * D_MODEL]) * y_b)
    out = x + _dot(mixed.astype(BF16), w_o_ref[...])
    return out, tail_a, tail_b, h_last


def _mixer_kernel(*refs, B, tT, n_tiles, Bs, raw_in):
    if raw_in:
        xraw_ref, meta_ref, *refs = refs
    else:
        x_ref, *refs = refs
    (xs_ref, ssa_ref, ssb_ref, ssh_ref, *refs) = refs
    w, refs = refs[:13], refs[13:]
    (out_ref, oa_ref, ob_ref, oh_ref, ys_ref, soa_ref, sob_ref, soh_ref, *refs) = refs
    sc, rest = refs[:5], refs[5:]
    exta, extb, _, h_s, _ = sc
    R = B * tT
    i = pl.program_id(0)
    na = (CONV_A_TAPS - 1) * B
    nb = (CONV_B_TAPS - 1) * B

    @pl.when(i == 0)
    def _():
        exta[0:na, :] = jnp.zeros((na, D_CONV), F32)
        extb[0:nb, :] = jnp.zeros((nb, D_RNN), F32)
        h_s[0:B, :] = jnp.zeros((B, D_RNN), F32)

    @pl.when(i < n_tiles)
    def _():
        if raw_in:
            (xt,) = rest
            nm_rows = N_META * B

            @pl.when(i == 0)
            def _():
                xt[0:nm_rows, :] = jnp.broadcast_to(
                    meta_ref[...][:, None, :], (N_META, B, D_MODEL)).reshape(nm_rows, D_MODEL)
                xt[nm_rows:R, :] = _to_time_major(xraw_ref[:, 0:tT - N_META, :])

            @pl.when(i > 0)
            def _():
                xt[...] = _to_time_major(xraw_ref[...])

            x = xt[...]
        else:
            x = x_ref[...]
        is_first = i == 0
        out, tail_a, tail_b, h_last = _mixer_tile(x, w, sc, B=B, tT=tT, is_first=is_first)
        out_ref[...] = out

        @pl.when(i == n_tiles - 1)
        def _():
            _store_state(oa_ref, tail_a, CONV_A_TAPS, B)
            _store_state(ob_ref, tail_b, CONV_B_TAPS, B)
            oh_ref[...] = h_last

    @pl.when(i == n_tiles)
    def _():
        _load_state(exta, ssa_ref, CONV_A_TAPS, Bs)
        _load_state(extb, ssb_ref, CONV_B_TAPS, Bs)
        h_s[0:Bs, :] = ssh_ref[...]
        out, tail_a, tail_b, h_last = _mixer_tile(xs_ref[...], w, sc, B=Bs, tT=1, is_first=None)
        ys_ref[...] = out
        _store_state(soa_ref, tail_a, CONV_A_TAPS, Bs)
        _store_state(sob_ref, tail_b, CONV_B_TAPS, Bs)
        soh_ref[...] = h_last


def _ffn_tile(x, w, extf, *, B, tT, final_norm):
    nf_ref, w_up_ref, w_gate_ref, fcw_ref, fcb_ref, w_down_ref, nfin_ref = w
    R = B * tT
    nf = (CONV_F_TAPS - 1) * B
    hf = _rmsnorm(x, nf_ref[...]).astype(BF16)
    extf[nf:nf + R, :] = _dot(hf, w_up_ref[...])
    uc = fcw_ref[0:1, :] * extf[0:R, :]
    for k in range(1, CONV_F_TAPS):
        uc = uc + fcw_ref[k:k + 1, :] * extf[k * B:k * B + R, :]
    uc = uc + fcb_ref[...]
    tail_f = extf[R:R + nf, :]
    extf[0:nf, :] = tail_f
    gate = _dot(hf, w_gate_ref[...])
    act = (uc * _sigmoid(uc) * gate).astype(BF16)
    y = x + _dot(act, w_down_ref[...])
    if final_norm:
        y = _rmsnorm(y, nfin_ref[...])
    return y, tail_f


def _ffn_kernel(x_ref, xs_ref, ssf_ref, *refs, B, tT, n_tiles, Bs, final_norm, raw_out):
    w, refs = refs[:7], refs[7:]
    out_ref, of_ref, ys_ref, sof_ref, extf, *rest = refs
    i = pl.program_id(0)
    nf = (CONV_F_TAPS - 1) * B

    @pl.when(i == 0)
    def _():
        extf[0:nf, :] = jnp.zeros((nf, D_FF), F32)

    @pl.when(i < n_tiles)
    def _():
        y, tail_f = _ffn_tile(x_ref[...], w, extf, B=B, tT=tT, final_norm=final_norm)
        if raw_out:
            obuf, sem = rest
            slot = lax.rem(i, 2)
            obuf[slot] = jnp.swapaxes(y.reshape(tT, B, D_MODEL), 0, 1)

            def head_copy():
                return pltpu.make_async_copy(obuf.at[0, :, N_META:tT, :],
                                             out_ref.at[:, 0:tT - N_META, :], sem.at[0])

            def tile_copy(step, slot_):
                start = pl.multiple_of(step * tT - N_META, N_META)
                return pltpu.make_async_copy(
                    obuf.at[slot_], out_ref.at[:, pl.ds(start, tT), :], sem.at[0])

            @pl.when(i == 1)
            def _():
                head_copy().wait()

            @pl.when(i > 1)
            def _():
                tile_copy(i - 1, 1 - slot).wait()

            @pl.when(i == 0)
            def _():
                head_copy().start()

            @pl.when(i > 0)
            def _():
                tile_copy(i, slot).start()

            @pl.when(i == n_tiles - 1)
            def _():
                tile_copy(i, slot).wait()
        else:
            out_ref[...] = y

        @pl.when(i == n_tiles - 1)
        def _():
            _store_state(of_ref, tail_f, CONV_F_TAPS, B)

    @pl.when(i == n_tiles)
    def _():
        _load_state(extf, ssf_ref, CONV_F_TAPS, Bs)
        y, tail_f = _ffn_tile(xs_ref[...], w, extf, B=Bs, tT=1, final_norm=final_norm)
        ys_ref[...] = y
        _store_state(sof_ref, tail_f, CONV_F_TAPS, Bs)


def _layer_spec(arr, l):
    nd = arr.ndim - 1
    return pl.BlockSpec((None,) + arr.shape[1:], lambda i: (l,) + (0,) * nd,
                        pipeline_mode=pl.Buffered(1))


def _whole_spec(arr):
    nd = arr.ndim
    return pl.BlockSpec(arr.shape, lambda i: (0,) * nd, pipeline_mode=pl.Buffered(1))


def _out_whole(shape):
    nd = len(shape)
    return pl.BlockSpec(shape, lambda i: (0,) * nd)


_PARAMS = pltpu.CompilerParams(dimension_semantics=("arbitrary",),
                               vmem_limit_bytes=VMEM_LIMIT_BYTES_V7X)


def _mixer_call(x, meta, xs, s_states, w, l, *, B, tT, rows, raw_in):
    R = B * tT
    n_tiles = rows // R
    assert n_tiles * R == rows
    Bs = xs.shape[0]
    body = functools.partial(_mixer_kernel, B=B, tT=tT, n_tiles=n_tiles, Bs=Bs, raw_in=raw_in)
    last = n_tiles - 1
    row_spec = pl.BlockSpec((R, D_MODEL), lambda i: (jnp.minimum(i, last), 0))
    if raw_in:
        assert tT % N_META == 0 and tT > N_META
        spt = tT // N_META
        x_specs = [pl.BlockSpec(
            (pl.Element(B), pl.Element(tT), pl.Element(D_MODEL)),
            lambda i: (0, jnp.maximum(jnp.minimum(i, last) * spt - 1, 0) * N_META, 0)),
            _whole_spec(meta)]
        x_args = [x, meta]
        x_scratch = [pltpu.VMEM((R, D_MODEL), F32)]
    else:
        x_specs, x_args, x_scratch = [row_spec], [x], []
    ssa, ssb, ssh = s_states
    layered = [ssa, ssb, ssh, w["norm_mix"], w["w_in"], w["b_gate"], w["conv_a_w"],
               w["w_a_out"], w["conv_b_w"], w["conv_b_b"], w["w_gates"], w["rg_b_a"],
               w["rg_b_x"], w["rg_lambda"], w["w_b_out"], w["w_o"]]
    st_shapes = lambda b: [(b, CONV_A_TAPS - 1, D_CONV), (b, CONV_B_TAPS - 1, D_RNN), (b, D_RNN)]
    out_shapes = [(rows, D_MODEL)] + st_shapes(B) + [(Bs, D_MODEL)] + st_shapes(Bs)
    ext_a_rows = max(R + (CONV_A_TAPS - 1) * B, CONV_A_TAPS * Bs)
    ext_b_rows = max(R + (CONV_B_TAPS - 1) * B, CONV_B_TAPS * Bs)
    rmax = max(R, Bs)
    return pl.pallas_call(
        body,
        grid=(n_tiles + 1,),
        in_specs=x_specs + [_whole_spec(xs)] + [_layer_spec(a, l) for a in layered],
        out_specs=[row_spec] + [_out_whole(s) for s in out_shapes[1:]],
        out_shape=[jax.ShapeDtypeStruct(s, F32) for s in out_shapes],
        scratch_shapes=[pltpu.VMEM((ext_a_rows, D_CONV), F32),
                        pltpu.VMEM((ext_b_rows, D_RNN), F32),
                        pltpu.VMEM((rmax, D_RNN), F32),
                        pltpu.VMEM((max(B, Bs), D_RNN), F32),
                        pltpu.VMEM((rmax, 2 * D_RNN), F32)] + x_scratch,
        compiler_params=_PARAMS,
        name="mixer",
    )(*x_args, xs, *layered)


def _ffn_call(x, xs, ssf, w, l, *, B, tT, rows, final_norm, raw_out, seq):
    R = B * tT
    n_tiles = rows // R
    assert n_tiles * R == rows
    Bs = xs.shape[0]
    body = functools.partial(_ffn_kernel, B=B, tT=tT, n_tiles=n_tiles, Bs=Bs,
                             final_norm=final_norm, raw_out=raw_out)
    last = n_tiles - 1
    row_spec = pl.BlockSpec((R, D_MODEL), lambda i: (jnp.minimum(i, last), 0))
    layered = [ssf, w["norm_ffn"], w["ffn_w_up"], w["ffn_w_gate"], w["ffn_conv_w"],
               w["ffn_conv_b"], w["ffn_w_down"]]
    if raw_out:
        assert n_tiles > 2 and tT % N_META == 0 and tT > N_META
        y_spec = pl.BlockSpec(memory_space=pl.ANY)
        y_shape = (B, seq, D_MODEL)
        y_scratch = [pltpu.VMEM((2, B, tT, D_MODEL), F32), pltpu.SemaphoreType.DMA((1,))]
    else:
        y_spec, y_shape, y_scratch = row_spec, (rows, D_MODEL), []
    out_shapes = [y_shape, (B, CONV_F_TAPS - 1, D_FF), (Bs, D_MODEL), (Bs, CONV_F_TAPS - 1, D_FF)]
    ext_f_rows = max(R + (CONV_F_TAPS - 1) * B, CONV_F_TAPS * Bs)
    return pl.pallas_call(
        body,
        grid=(n_tiles + 1,),
        in_specs=[row_spec, _whole_spec(xs)] + [_layer_spec(a, l) for a in layered]
                 + [_whole_spec(w["norm_final"])],
        out_specs=[y_spec] + [_out_whole(s) for s in out_shapes[1:]],
        out_shape=[jax.ShapeDtypeStruct(s, F32) for s in out_shapes],
        scratch_shapes=[pltpu.VMEM((ext_f_rows, D_FF), F32)] + y_scratch,
        compiler_params=_PARAMS,
        name="ffn",
    )(x, xs, *layered, w["norm_final"])


def _dense_gate_weights(rg_w_a, rg_w_x):
    hpb = MXU_EDGE_V7X // RG_HEAD_DIM
    idx = jnp.arange(MXU_EDGE_V7X) // RG_HEAD_DIM
    on_diag = idx[:, None] == idx[None, :]

    def dense(wt):
        wt = wt.reshape(DEPTH, GATE_BLOCKS, hpb, RG_HEAD_DIM, RG_HEAD_DIM)
        wide = wt.transpose(0, 1, 3, 2, 4).reshape(DEPTH, GATE_BLOCKS, RG_HEAD_DIM, MXU_EDGE_V7X)
        return jnp.where(on_diag, 0.5 * jnp.tile(wide, (1, 1, hpb, 1)), 0.0)

    return jnp.concatenate([dense(rg_w_a), dense(rg_w_x)], axis=-1).astype(BF16)


def kernel(x_prompt, x_sample, state_conv_a, state_conv_b, state_rglru, state_conv_ffn,
           meta_tokens, norm_mix, norm_ffn, norm_final, w_in, b_gate, conv_a_w, w_a_out,
           conv_b_w, conv_b_b, rg_w_a, rg_b_a, rg_w_x, rg_b_x, rg_lambda, w_b_out, w_o,
           ffn_w_up, ffn_w_gate, ffn_conv_w, ffn_conv_b, ffn_w_down):
    assert PAST_LEN > 0
    vec = lambda v: v.reshape(DEPTH, 1, -1)
    w = dict(
        norm_mix=vec(norm_mix), norm_ffn=vec(norm_ffn), norm_final=norm_final.reshape(1, -1),
        w_in=w_in.astype(BF16), b_gate=vec(b_gate), conv_a_w=conv_a_w,
        w_a_out=w_a_out.astype(BF16), conv_b_w=conv_b_w, conv_b_b=vec(conv_b_b),
        w_gates=_dense_gate_weights(rg_w_a, rg_w_x), rg_b_a=vec(rg_b_a), rg_b_x=vec(rg_b_x),
        rg_lambda=vec(rg_lambda), w_b_out=w_b_out.astype(BF16), w_o=w_o.astype(BF16),
        ffn_w_up=ffn_w_up.astype(BF16), ffn_w_gate=ffn_w_gate.astype(BF16),
        ffn_conv_w=ffn_conv_w, ffn_conv_b=vec(ffn_conv_b), ffn_w_down=ffn_w_down.astype(BF16))

    bp, seq, _ = x_prompt.shape
    dt = x_prompt.dtype
    bs, t_s, _ = x_sample.shape
    assert t_s == 1
    rows = (N_META + seq) * bp
    x = x_prompt
    xs = x_sample.reshape(bs, D_MODEL)
    meta = meta_tokens.astype(dt)
    p_a, p_b, p_h, p_f, s_a, s_b, s_h, s_f = ([] for _ in range(8))
    for l in range(DEPTH):
        x, pa, pb, ph, xs, sa, sb, sh = _mixer_call(
            x, meta, xs, (state_conv_a, state_conv_b, state_rglru), w, l,
            B=bp, tT=PROMPT_TILE_STEPS, rows=rows, raw_in=(l == 0))
        last = l == DEPTH - 1
        x, pf, xs, sf = _ffn_call(x, xs, state_conv_ffn, w, l, B=bp,
                                  tT=PROMPT_TILE_STEPS if last else FFN_TILE_STEPS,
                                  rows=rows, final_norm=last, raw_out=last, seq=seq)
        for acc, v in zip((p_a, p_b, p_h, p_f, s_a, s_b, s_h, s_f),
                          (pa, pb, ph, pf, sa, sb, sh, sf)):
            acc.append(v)
    y_sample = xs.reshape(bs, t_s, D_MODEL)
    return (x, y_sample, jnp.stack(p_a), jnp.stack(p_b), jnp.stack(p_h), jnp.stack(p_f),
            jnp.stack(s_a), jnp.stack(s_b), jnp.stack(s_h), jnp.stack(s_f))
```

```python
import functools

import jax
import jax.numpy as jnp
from jax import lax
from jax.experimental import pallas as pl
from jax.experimental.pallas import tpu as pltpu

D_MODEL = 1024
DEPTH = 4
N_META = 16
D_CONV = D_MODEL
D_RNN = D_MODEL
RG_HEADS = 16
RG_HEAD_DIM = D_RNN // RG_HEADS
RG_C = 8.0
D_FF = 2816
EPS = 1e-6
PAST_LEN = 16384
CONV_A_TAPS = 3
CONV_B_TAPS = 4
CONV_F_TAPS = 3

MXU_EDGE_V7X = 256
GATE_BLOCKS = D_RNN // MXU_EDGE_V7X
VMEM_LIMIT_BYTES_V7X = 58 * 1024 * 1024
PROMPT_TILE_STEPS = 48
FFN_TILE_STEPS = 86

F32 = jnp.float32
BF16 = jnp.bfloat16


def _dot(a, b):
    return jnp.dot(a, b, preferred_element_type=F32)


def _rmsnorm(x, g):
    ms = jnp.mean(x * x, axis=-1, keepdims=True)
    return x * lax.rsqrt(ms + EPS) * g


def _sigmoid(z):
    return 0.5 * jnp.tanh(0.5 * z) + 0.5


def _softplus(z):
    return jnp.maximum(z, 0.0) + jnp.log1p(jnp.exp(-jnp.abs(z)))


def _load_state(ext, s_ref, taps, B):
    for k in range(taps - 1):
        ext[k * B:(k + 1) * B, :] = s_ref[:, k, :]


def _store_state(o_ref, tail, taps, B):
    for k in range(taps - 1):
        o_ref[:, k, :] = tail[k * B:(k + 1) * B, :]


def _mixer_tile(x, w, sc, *, B, tT, is_first):
    (nm_ref, w_in_ref, bg_ref, caw_ref, w_a_out_ref, cbw_ref, cbb_ref, wg_ref, bga_ref,
     bgx_ref, lam_ref, w_b_out_ref, w_o_ref) = w
    exta, extb, b_s, h_s, g_s = sc
    R = B * tT
    na = (CONV_A_TAPS - 1) * B
    nb = (CONV_B_TAPS - 1) * B
    hn = _rmsnorm(x, nm_ref[...]).astype(BF16)

    o_x = 3 * D_CONV
    half_c = (0.5 * RG_C) * _softplus(-lam_ref[...])
    half_ba = 0.5 * bga_ref[...]
    half_bx = 0.5 * bgx_ref[...]
    blk = lambda j: slice(j * MXU_EDGE_V7X, (j + 1) * MXU_EDGE_V7X)
    extb[nb:nb + R, :] = _dot(hn, w_in_ref[:, o_x:o_x + D_RNN])
    wide_a = _dot(hn, w_in_ref[:, 0:3 * D_CONV])
    xc = cbw_ref[0:1, :] * extb[0:R, :]
    for k in range(1, CONV_B_TAPS):
        xc = xc + cbw_ref[k:k + 1, :] * extb[k * B:k * B + R, :]
    xc = xc + cbb_ref[...]
    tail_b = extb[R:R + nb, :]
    extb[0:nb, :] = tail_b
    xcb = xc.astype(BF16)
    for j in range(GATE_BLOCKS):
        g = _dot(xcb[:, blk(j)], wg_ref[j])
        g_s[0:R, blk(j)] = g[:, :MXU_EDGE_V7X]
        g_s[0:R, D_RNN + j * MXU_EDGE_V7X:D_RNN + (j + 1) * MXU_EDGE_V7X] = g[:, MXU_EDGE_V7X:]
    o_r = 3 * D_CONV + D_RNN
    wide_b = _dot(hn, w_in_ref[:, o_r:o_r + 3 * D_MODEL])

    steps = min(tT, 8)
    assert tT % steps == 0
    rc = steps * B
    h = h_s[0:B, :]
    for k in range(tT // steps):
        r = slice(k * rc, (k + 1) * rc)
        t_r = jnp.tanh(g_s[r, 0:D_RNN] + half_ba)
        neg_log_a = half_c * t_r + half_c
        t_x = jnp.tanh(g_s[r, D_RNN:2 * D_RNN] + half_bx)
        s = jnp.tanh(neg_log_a)
        half_mult = jnp.where(s > 0.0, s * lax.rsqrt((s + s) * (s + 1.0)), 0.0)
        if is_first is not None and k == 0:
            half_mult = jnp.concatenate(
                [jnp.where(is_first, 0.5, half_mult[0:B]), half_mult[B:]], axis=0)
        a = jnp.exp(-neg_log_a)
        q = half_mult * xc[r, :]
        b = q * t_x + q
        for t in range(steps):
            h = a[t * B:(t + 1) * B] * h + b[t * B:(t + 1) * B]
            b_s[k * rc + t * B:k * rc + (t + 1) * B, :] = h
    h_s[0:B, :] = h
    h_last = h

    exta[na:na + R, :] = wide_a[:, D_CONV:2 * D_CONV] * wide_a[:, 2 * D_CONV:3 * D_CONV]
    conv_a = caw_ref[0:1, :] * exta[0:R, :]
    for k in range(1, CONV_A_TAPS):
        conv_a = conv_a + caw_ref[k:k + 1, :] * exta[k * B:k * B + R, :]
    y_a = _dot((wide_a[:, 0:D_CONV] * conv_a).astype(BF16), w_a_out_ref[...])
    tail_a = exta[R:R + na, :]
    exta[0:na, :] = tail_a

    y_b = _dot((jax.nn.gelu(wide_b[:, 0:D_RNN], approximate=True) * b_s[0:R, :]).astype(BF16),
               w_b_out_ref[...])

    mixed = (_sigmoid(wide_b[:, D_RNN:D_RNN + D_MODEL] + bg_ref[:, 0:D_MODEL]) * y_a
             + _sigmoid(wide_b[:, D_RNN + D_MODEL:] + bg_ref[:, D_MODEL:2 * D_MODEL]) * y_b)
    out = x + _dot(mixed.astype(BF16), w_o_ref[...])
    return out, tail_a, tail_b, h_last


def _mixer_kernel(*refs, B, tT, n_tiles, Bs, raw_in):
    if raw_in:
        xraw_ref, meta_ref, *refs = refs
    else:
        x_ref, *refs = refs
    (xs_ref, ssa_ref, ssb_ref, ssh_ref, *refs) = refs
    w, refs = refs[:13], refs[13:]
    (out_ref, oa_ref, ob_ref, oh_ref, ys_ref, soa_ref, sob_ref, soh_ref, *refs) = refs
    sc, rest = refs[:5], refs[5:]
    exta, extb, _, h_s, _ = sc
    R = B * tT
    i = pl.program_id(0)
    na = (CONV_A_TAPS - 1) * B
    nb = (CONV_B_TAPS - 1) * B

    @pl.when(i == 0)
    def _():
        exta[0:na, :] = jnp.zeros((na, D_CONV), F32)
        extb[0:nb, :] = jnp.zeros((nb, D_RNN), F32)
        h_s[0:B, :] = jnp.zeros((B, D_RNN), F32)

    if raw_in:
        xt, xsem = rest

        def head_fetch():
            return [pltpu.make_async_copy(xraw_ref.at[b, 0:tT - N_META, :],
                                          xt.at[0, N_META:tT, b, :], xsem.at[0])
                    for b in range(B)]

        def tile_fetch(step):
            start = pl.multiple_of(step * tT - N_META, N_META)
            slot_ = lax.rem(step, 2)
            return [pltpu.make_async_copy(xraw_ref.at[b, pl.ds(start, tT), :],
                                          xt.at[slot_, :, b, :], xsem.at[slot_])
                    for b in range(B)]

        @pl.when(i == 0)
        def _():
            for c in head_fetch():
                c.start()
            xt[0, 0:N_META, :, :] = jnp.broadcast_to(meta_ref[...][:, None, :],
                                                     (N_META, B, D_MODEL))

        @pl.when(i + 1 < n_tiles)
        def _():
            for c in tile_fetch(i + 1):
                c.start()

        @pl.when(i == 0)
        def _():
            for c in head_fetch():
                c.wait()

        @pl.when(jnp.logical_and(i > 0, i < n_tiles))
        def _():
            for c in tile_fetch(i):
                c.wait()

    @pl.when(i < n_tiles)
    def _():
        if raw_in:
            x = xt[lax.rem(i, 2)].reshape(R, D_MODEL)
        else:
            x = x_ref[...]
        is_first = i == 0
        out, tail_a, tail_b, h_last = _mixer_tile(x, w, sc, B=B, tT=tT, is_first=is_first)
        out_ref[...] = out

        @pl.when(i == n_tiles - 1)
        def _():
            _store_state(oa_ref, tail_a, CONV_A_TAPS, B)
            _store_state(ob_ref, tail_b, CONV_B_TAPS, B)
            oh_ref[...] = h_last

    @pl.when(i == n_tiles)
    def _():
        _load_state(exta, ssa_ref, CONV_A_TAPS, Bs)
        _load_state(extb, ssb_ref, CONV_B_TAPS, Bs)
        h_s[0:Bs, :] = ssh_ref[...]
        out, tail_a, tail_b, h_last = _mixer_tile(xs_ref[...], w, sc, B=Bs, tT=1, is_first=None)
        ys_ref[...] = out
        _store_state(soa_ref, tail_a, CONV_A_TAPS, Bs)
        _store_state(sob_ref, tail_b, CONV_B_TAPS, Bs)
        soh_ref[...] = h_last


def _ffn_tile(x, w, extf, *, B, tT, final_norm):
    nf_ref, w_up_ref, w_gate_ref, fcw_ref, fcb_ref, w_down_ref, nfin_ref = w
    R = B * tT
    nf = (CONV_F_TAPS - 1) * B
    hf = _rmsnorm(x, nf_ref[...]).astype(BF16)
    extf[nf:nf + R, :] = _dot(hf, w_up_ref[...])
    uc = fcw_ref[0:1, :] * extf[0:R, :]
    for k in range(1, CONV_F_TAPS):
        uc = uc + fcw_ref[k:k + 1, :] * extf[k * B:k * B + R, :]
    uc = uc + fcb_ref[...]
    tail_f = extf[R:R + nf, :]
    extf[0:nf, :] = tail_f
    gate = _dot(hf, w_gate_ref[...])
    act = (uc * _sigmoid(uc) * gate).astype(BF16)
    y = x + _dot(act, w_down_ref[...])
    if final_norm:
        y = _rmsnorm(y, nfin_ref[...])
    return y, tail_f


def _ffn_kernel(x_ref, xs_ref, ssf_ref, *refs, B, tT, n_tiles, Bs, final_norm, raw_out):
    w, refs = refs[:7], refs[7:]
    out_ref, of_ref, ys_ref, sof_ref, extf, *rest = refs
    i = pl.program_id(0)
    nf = (CONV_F_TAPS - 1) * B

    @pl.when(i == 0)
    def _():
        extf[0:nf, :] = jnp.zeros((nf, D_FF), F32)

    @pl.when(i < n_tiles)
    def _():
        y, tail_f = _ffn_tile(x_ref[...], w, extf, B=B, tT=tT, final_norm=final_norm)
        if raw_out:
            obuf, sem = rest
            slot = lax.rem(i, 2)
            obuf[slot] = y.reshape(tT, B, D_MODEL)

            def head_copies():
                return [pltpu.make_async_copy(obuf.at[0, N_META:tT, b, :],
                                              out_ref.at[b, 0:tT - N_META, :], sem.at[0])
                        for b in range(B)]

            def tile_copies(step, slot_):
                start = pl.multiple_of(step * tT - N_META, N_META)
                return [pltpu.make_async_copy(obuf.at[slot_, :, b, :],
                                              out_ref.at[b, pl.ds(start, tT), :], sem.at[0])
                        for b in range(B)]

            @pl.when(i == 1)
            def _():
                for c in head_copies():
                    c.wait()

            @pl.when(i > 1)
            def _():
                for c in tile_copies(i - 1, 1 - slot):
                    c.wait()

            @pl.when(i == 0)
            def _():
                for c in head_copies():
                    c.start()

            @pl.when(i > 0)
            def _():
                for c in tile_copies(i, slot):
                    c.start()

            @pl.when(i == n_tiles - 1)
            def _():
                for c in tile_copies(i, slot):
                    c.wait()
        else:
            out_ref[...] = y

        @pl.when(i == n_tiles - 1)
        def _():
            _store_state(of_ref, tail_f, CONV_F_TAPS, B)

    @pl.when(i == n_tiles)
    def _():
        _load_state(extf, ssf_ref, CONV_F_TAPS, Bs)
        y, tail_f = _ffn_tile(xs_ref[...], w, extf, B=Bs, tT=1, final_norm=final_norm)
        ys_ref[...] = y
        _store_state(sof_ref, tail_f, CONV_F_TAPS, Bs)


def _layer_spec(arr, l):
    nd = arr.ndim - 1
    return pl.BlockSpec((None,) + arr.shape[1:], lambda i: (l,) + (0,) * nd,
                        pipeline_mode=pl.Buffered(1))


def _whole_spec(arr):
    nd = arr.ndim
    return pl.BlockSpec(arr.shape, lambda i: (0,) * nd, pipeline_mode=pl.Buffered(1))


def _out_whole(shape):
    nd = len(shape)
    return pl.BlockSpec(shape, lambda i: (0,) * nd)


_PARAMS = pltpu.CompilerParams(dimension_semantics=("arbitrary",),
                               vmem_limit_bytes=VMEM_LIMIT_BYTES_V7X)


def _mixer_call(x, meta, xs, s_states, w, l, *, B, tT, rows, raw_in):
    R = B * tT
    n_tiles = rows // R
    assert n_tiles * R == rows
    Bs = xs.shape[0]
    body = functools.partial(_mixer_kernel, B=B, tT=tT, n_tiles=n_tiles, Bs=Bs, raw_in=raw_in)
    last = n_tiles - 1
    row_spec = pl.BlockSpec((R, D_MODEL), lambda i: (jnp.minimum(i, last), 0))
    if raw_in:
        assert n_tiles > 1 and tT % N_META == 0 and tT > N_META
        x_specs = [pl.BlockSpec(memory_space=pl.ANY), _whole_spec(meta)]
        x_args = [x, meta]
        x_scratch = [pltpu.VMEM((2, tT, B, D_MODEL), F32), pltpu.SemaphoreType.DMA((2,))]
    else:
        x_specs, x_args, x_scratch = [row_spec], [x], []
    ssa, ssb, ssh = s_states
    layered = [ssa, ssb, ssh, w["norm_mix"], w["w_in"], w["b_gate"], w["conv_a_w"],
               w["w_a_out"], w["conv_b_w"], w["conv_b_b"], w["w_gates"], w["rg_b_a"],
               w["rg_b_x"], w["rg_lambda"], w["w_b_out"], w["w_o"]]
    st_shapes = lambda b: [(b, CONV_A_TAPS - 1, D_CONV), (b, CONV_B_TAPS - 1, D_RNN), (b, D_RNN)]
    out_shapes = [(rows, D_MODEL)] + st_shapes(B) + [(Bs, D_MODEL)] + st_shapes(Bs)
    ext_a_rows = max(R + (CONV_A_TAPS - 1) * B, CONV_A_TAPS * Bs)
    ext_b_rows = max(R + (CONV_B_TAPS - 1) * B, CONV_B_TAPS * Bs)
    rmax = max(R, Bs)
    return pl.pallas_call(
        body,
        grid=(n_tiles + 1,),
        in_specs=x_specs + [_whole_spec(xs)] + [_layer_spec(a, l) for a in layered],
        out_specs=[row_spec] + [_out_whole(s) for s in out_shapes[1:]],
        out_shape=[jax.ShapeDtypeStruct(s, F32) for s in out_shapes],
        scratch_shapes=[pltpu.VMEM((ext_a_rows, D_CONV), F32),
                        pltpu.VMEM((ext_b_rows, D_RNN), F32),
                        pltpu.VMEM((rmax, D_RNN), F32),
                        pltpu.VMEM((max(B, Bs), D_RNN), F32),
                        pltpu.VMEM((rmax, 2 * D_RNN), F32)] + x_scratch,
        compiler_params=_PARAMS,
        name="mixer",
    )(*x_args, xs, *layered)


def _ffn_call(x, xs, ssf, w, l, *, B, tT, rows, final_norm, raw_out, seq):
    R = B * tT
    n_tiles = rows // R
    assert n_tiles * R == rows
    Bs = xs.shape[0]
    body = functools.partial(_ffn_kernel, B=B, tT=tT, n_tiles=n_tiles, Bs=Bs,
                             final_norm=final_norm, raw_out=raw_out)
    last = n_tiles - 1
    row_spec = pl.BlockSpec((R, D_MODEL), lambda i: (jnp.minimum(i, last), 0))
    layered = [ssf, w["norm_ffn"], w["ffn_w_up"], w["ffn_w_gate"], w["ffn_conv_w"],
               w["ffn_conv_b"], w["ffn_w_down"]]
    if raw_out:
        assert n_tiles > 2 and tT % N_META == 0 and tT > N_META
        y_spec = pl.BlockSpec(memory_space=pl.ANY)
        y_shape = (B, seq, D_MODEL)
        y_scratch = [pltpu.VMEM((2, tT, B, D_MODEL), F32), pltpu.SemaphoreType.DMA((1,))]
    else:
        y_spec, y_shape, y_scratch = row_spec, (rows, D_MODEL), []
    out_shapes = [y_shape, (B, CONV_F_TAPS - 1, D_FF), (Bs, D_MODEL), (Bs, CONV_F_TAPS - 1, D_FF)]
    ext_f_rows = max(R + (CONV_F_TAPS - 1) * B, CONV_F_TAPS * Bs)
    return pl.pallas_call(
        body,
        grid=(n_tiles + 1,),
        in_specs=[row_spec, _whole_spec(xs)] + [_layer_spec(a, l) for a in layered]
                 + [_whole_spec(w["norm_final"])],
        out_specs=[y_spec] + [_out_whole(s) for s in out_shapes[1:]],
        out_shape=[jax.ShapeDtypeStruct(s, F32) for s in out_shapes],
        scratch_shapes=[pltpu.VMEM((ext_f_rows, D_FF), F32)] + y_scratch,
        compiler_params=_PARAMS,
        name="ffn",
    )(x, xs, *layered, w["norm_final"])


def _dense_gate_weights(rg_w_a, rg_w_x):
    hpb = MXU_EDGE_V7X // RG_HEAD_DIM
    idx = jnp.arange(MXU_EDGE_V7X) // RG_HEAD_DIM
    on_diag = idx[:, None] == idx[None, :]

    def dense(wt):
        wt = wt.reshape(DEPTH, GATE_BLOCKS, hpb, RG_HEAD_DIM, RG_HEAD_DIM)
        wide = wt.transpose(0, 1, 3, 2, 4).reshape(DEPTH, GATE_BLOCKS, RG_HEAD_DIM, MXU_EDGE_V7X)
        return jnp.where(on_diag, 0.5 * jnp.tile(wide, (1, 1, hpb, 1)), 0.0)

    return jnp.concatenate([dense(rg_w_a), dense(rg_w_x)], axis=-1).astype(BF16)


def kernel(x_prompt, x_sample, state_conv_a, state_conv_b, state_rglru, state_conv_ffn,
           meta_tokens, norm_mix, norm_ffn, norm_final, w_in, b_gate, conv_a_w, w_a_out,
           conv_b_w, conv_b_b, rg_w_a, rg_b_a, rg_w_x, rg_b_x, rg_lambda, w_b_out, w_o,
           ffn_w_up, ffn_w_gate, ffn_conv_w, ffn_conv_b, ffn_w_down):
    assert PAST_LEN > 0
    vec = lambda v: v.reshape(DEPTH, 1, -1)
    w = dict(
        norm_mix=vec(norm_mix), norm_ffn=vec(norm_ffn), norm_final=norm_final.reshape(1, -1),
        w_in=w_in.astype(BF16), b_gate=vec(b_gate), conv_a_w=conv_a_w,
        w_a_out=w_a_out.astype(BF16), conv_b_w=conv_b_w, conv_b_b=vec(conv_b_b),
        w_gates=_dense_gate_weights(rg_w_a, rg_w_x), rg_b_a=vec(rg_b_a), rg_b_x=vec(rg_b_x),
        rg_lambda=vec(rg_lambda), w_b_out=w_b_out.astype(BF16), w_o=w_o.astype(BF16),
        ffn_w_up=ffn_w_up.astype(BF16), ffn_w_gate=ffn_w_gate.astype(BF16),
        ffn_conv_w=ffn_conv_w, ffn_conv_b=vec(ffn_conv_b), ffn_w_down=ffn_w_down.astype(BF16))

    bp, seq, _ = x_prompt.shape
    dt = x_prompt.dtype
    bs, t_s, _ = x_sample.shape
    assert t_s == 1
    rows = (N_META + seq) * bp
    x = x_prompt
    xs = x_sample.reshape(bs, D_MODEL)
    meta = meta_tokens.astype(dt)
    p_a, p_b, p_h, p_f, s_a, s_b, s_h, s_f = ([] for _ in range(8))
    for l in range(DEPTH):
        x, pa, pb, ph, xs, sa, sb, sh = _mixer_call(
            x, meta, xs, (state_conv_a, state_conv_b, state_rglru), w, l,
            B=bp, tT=PROMPT_TILE_STEPS, rows=rows, raw_in=(l == 0))
        last = l == DEPTH - 1
        x, pf, xs, sf = _ffn_call(x, xs, state_conv_ffn, w, l, B=bp,
                                  tT=PROMPT_TILE_STEPS if last else FFN_TILE_STEPS,
                                  rows=rows, final_norm=last, raw_out=last, seq=seq)
        for acc, v in zip((p_a, p_b, p_h, p_f, s_a, s_b, s_h, s_f),
                          (pa, pb, ph, pf, sa, sb, sh, sf)):
            acc.append(v)
    y_sample = xs.reshape(bs, t_s, D_MODEL)
    return (x, y_sample, jnp.stack(p_a), jnp.stack(p_b), jnp.stack(p_h), jnp.stack(p_f),
            jnp.stack(s_a), jnp.stack(s_b), jnp.stack(s_h), jnp.stack(s_f))
```

```python
import functools

import jax
import jax.numpy as jnp
from jax import lax
from jax.experimental import pallas as pl
from jax.experimental.pallas import tpu as pltpu

D_MODEL = 1024
DEPTH = 4
N_META = 16
D_CONV = D_MODEL
D_RNN = D_MODEL
RG_HEADS = 16
RG_HEAD_DIM = D_RNN // RG_HEADS
RG_C = 8.0
D_FF = 2816
EPS = 1e-6
PAST_LEN = 16384
CONV_A_TAPS = 3
CONV_B_TAPS = 4
CONV_F_TAPS = 3

MXU_EDGE_V7X = 256
GATE_BLOCKS = D_RNN // MXU_EDGE_V7X
VMEM_LIMIT_BYTES_V7X = 58 * 1024 * 1024
PROMPT_TILE_STEPS = 48
FFN_TILE_STEPS = 86

F32 = jnp.float32
BF16 = jnp.bfloat16


def _dot(a, b):
    return jnp.dot(a, b, preferred_element_type=F32)


def _rmsnorm(x, g):
    ms = jnp.mean(x * x, axis=-1, keepdims=True)
    return x * lax.rsqrt(ms + EPS) * g


def _sigmoid(z):
    return 0.5 * jnp.tanh(0.5 * z) + 0.5


def _softplus(z):
    return jnp.maximum(z, 0.0) + jnp.log1p(jnp.exp(-jnp.abs(z)))


def _load_state(ext, s_ref, taps, B):
    for k in range(taps - 1):
        ext[k * B:(k + 1) * B, :] = s_ref[:, k, :]


def _store_state(o_ref, tail, taps, B):
    for k in range(taps - 1):
        o_ref[:, k, :] = tail[k * B:(k + 1) * B, :]


def _mixer_tile(x, w, sc, *, B, tT, is_first):
    (nm_ref, w_in_ref, bg_ref, caw_ref, w_a_out_ref, cbw_ref, cbb_ref, wg_ref, bga_ref,
     bgx_ref, lam_ref, w_b_out_ref, w_o_ref) = w
    exta, extb, b_s, h_s, g_s = sc
    R = B * tT
    na = (CONV_A_TAPS - 1) * B
    nb = (CONV_B_TAPS - 1) * B
    hn = _rmsnorm(x, nm_ref[...]).astype(BF16)

    o_x = 3 * D_CONV
    half_c = (0.5 * RG_C) * _softplus(-lam_ref[...])
    half_ba = 0.5 * bga_ref[...]
    half_bx = 0.5 * bgx_ref[...]
    blk = lambda j: slice(j * MXU_EDGE_V7X, (j + 1) * MXU_EDGE_V7X)
    extb[nb:nb + R, :] = _dot(hn, w_in_ref[:, o_x:o_x + D_RNN])
    wide_a = _dot(hn, w_in_ref[:, 0:3 * D_CONV])
    xc = cbw_ref[0:1, :] * extb[0:R, :]
    for k in range(1, CONV_B_TAPS):
        xc = xc + cbw_ref[k:k + 1, :] * extb[k * B:k * B + R, :]
    xc = xc + cbb_ref[...]
    tail_b = extb[R:R + nb, :]
    extb[0:nb, :] = tail_b
    xcb = xc.astype(BF16)
    for j in range(GATE_BLOCKS):
        g = _dot(xcb[:, blk(j)], wg_ref[j])
        g_s[0:R, blk(j)] = g[:, :MXU_EDGE_V7X]
        g_s[0:R, D_RNN + j * MXU_EDGE_V7X:D_RNN + (j + 1) * MXU_EDGE_V7X] = g[:, MXU_EDGE_V7X:]
    o_r = 3 * D_CONV + D_RNN
    wide_b = _dot(hn, w_in_ref[:, o_r:o_r + 3 * D_MODEL])

    steps = min(tT, 8)
    assert tT % steps == 0
    rc = steps * B
    h = h_s[0:B, :]
    for k in range(tT // steps):
        r = slice(k * rc, (k + 1) * rc)
        t_r = jnp.tanh(g_s[r, 0:D_RNN] + half_ba)
        neg_log_a = half_c * t_r + half_c
        t_x = jnp.tanh(g_s[r, D_RNN:2 * D_RNN] + half_bx)
        s = jnp.tanh(neg_log_a)
        half_mult = jnp.where(s > 0.0, s * lax.rsqrt((s + s) * (s + 1.0)), 0.0)
        if is_first is not None and k == 0:
            half_mult = jnp.concatenate(
                [jnp.where(is_first, 0.5, half_mult[0:B]), half_mult[B:]], axis=0)
        a = jnp.exp(-neg_log_a)
        q = half_mult * xc[r, :]
        b = q * t_x + q
        for t in range(steps):
            h = a[t * B:(t + 1) * B] * h + b[t * B:(t + 1) * B]
            b_s[k * rc + t * B:k * rc + (t + 1) * B, :] = h
    h_s[0:B, :] = h
    h_last = h

    exta[na:na + R, :] = wide_a[:, D_CONV:2 * D_CONV] * wide_a[:, 2 * D_CONV:3 * D_CONV]
    conv_a = caw_ref[0:1, :] * exta[0:R, :]
    for k in range(1, CONV_A_TAPS):
        conv_a = conv_a + caw_ref[k:k + 1, :] * exta[k * B:k * B + R, :]
    y_a = _dot((wide_a[:, 0:D_CONV] * conv_a).astype(BF16), w_a_out_ref[...])
    tail_a = exta[R:R + na, :]
    exta[0:na, :] = tail_a

    y_b = _dot((jax.nn.gelu(wide_b[:, 0:D_RNN], approximate=True) * b_s[0:R, :]).astype(BF16),
               w_b_out_ref[...])

    mixed = (_sigmoid(wide_b[:, D_RNN:D_RNN + D_MODEL] + bg_ref[:, 0:D_MODEL]) * y_a
             + _sigmoid(wide_b[:, D_RNN + D_MODEL:] + bg_ref[:, D_MODEL:2 * D_MODEL]) * y_b)
    out = x + _dot(mixed.astype(BF16), w_o_ref[...])
    return out, tail_a, tail_b, h_last


def _mixer_kernel(*refs, B, tT, n_tiles, Bs, raw_in):
    if raw_in:
        xraw_ref, meta_ref, *refs = refs
    else:
        x_ref, *refs = refs
    (xs_ref, ssa_ref, ssb_ref, ssh_ref, *refs) = refs
    w, refs = refs[:13], refs[13 + 6:]
    (out_ref, oa_ref, ob_ref, oh_ref, ys_ref, soa_ref, sob_ref, soh_ref, *refs) = refs
    sc, rest = refs[:5], refs[5:]
    exta, extb, _, h_s, _ = sc
    R = B * tT
    i = pl.program_id(0)
    na = (CONV_A_TAPS - 1) * B
    nb = (CONV_B_TAPS - 1) * B

    @pl.when(i == 0)
    def _():
        exta[0:na, :] = jnp.zeros((na, D_CONV), F32)
        extb[0:nb, :] = jnp.zeros((nb, D_RNN), F32)
        h_s[0:B, :] = jnp.zeros((B, D_RNN), F32)

    if raw_in:
        xt, xsem = rest

        def head_fetch():
            return [pltpu.make_async_copy(xraw_ref.at[b, 0:tT - N_META, :],
                                          xt.at[0, N_META:tT, b, :], xsem.at[0])
                    for b in range(B)]

        def tile_fetch(step):
            start = pl.multiple_of(step * tT - N_META, N_META)
            slot_ = lax.rem(step, 2)
            return [pltpu.make_async_copy(xraw_ref.at[b, pl.ds(start, tT), :],
                                          xt.at[slot_, :, b, :], xsem.at[slot_])
                    for b in range(B)]

        @pl.when(i == 0)
        def _():
            for c in head_fetch():
                c.start()
            xt[0, 0:N_META, :, :] = jnp.broadcast_to(meta_ref[...][:, None, :],
                                                     (N_META, B, D_MODEL))

        @pl.when(i + 1 < n_tiles)
        def _():
            for c in tile_fetch(i + 1):
                c.start()

        @pl.when(i == 0)
        def _():
            for c in head_fetch():
                c.wait()

        @pl.when(jnp.logical_and(i > 0, i < n_tiles))
        def _():
            for c in tile_fetch(i):
                c.wait()

    @pl.when(i < n_tiles)
    def _():
        if raw_in:
            x = xt[lax.rem(i, 2)].reshape(R, D_MODEL)
        else:
            x = x_ref[...]
        is_first = i == 0
        out, tail_a, tail_b, h_last = _mixer_tile(x, w, sc, B=B, tT=tT, is_first=is_first)
        out_ref[...] = out

        @pl.when(i == n_tiles - 1)
        def _():
            _store_state(oa_ref, tail_a, CONV_A_TAPS, B)
            _store_state(ob_ref, tail_b, CONV_B_TAPS, B)
            oh_ref[...] = h_last

    @pl.when(i == n_tiles)
    def _():
        _load_state(exta, ssa_ref, CONV_A_TAPS, Bs)
        _load_state(extb, ssb_ref, CONV_B_TAPS, Bs)
        h_s[0:Bs, :] = ssh_ref[...]
        out, tail_a, tail_b, h_last = _mixer_tile(xs_ref[...], w, sc, B=Bs, tT=1, is_first=None)
        ys_ref[...] = out
        _store_state(soa_ref, tail_a, CONV_A_TAPS, Bs)
        _store_state(sob_ref, tail_b, CONV_B_TAPS, Bs)
        soh_ref[...] = h_last


def _ffn_tile(x, w, extf, *, B, tT, final_norm):
    nf_ref, w_up_ref, w_gate_ref, fcw_ref, fcb_ref, w_down_ref, nfin_ref = w
    R = B * tT
    nf = (CONV_F_TAPS - 1) * B
    hf = _rmsnorm(x, nf_ref[...]).astype(BF16)
    extf[nf:nf + R, :] = _dot(hf, w_up_ref[...])
    uc = fcw_ref[0:1, :] * extf[0:R, :]
    for k in range(1, CONV_F_TAPS):
        uc = uc + fcw_ref[k:k + 1, :] * extf[k * B:k * B + R, :]
    uc = uc + fcb_ref[...]
    tail_f = extf[R:R + nf, :]
    extf[0:nf, :] = tail_f
    gate = _dot(hf, w_gate_ref[...])
    act = (uc * _sigmoid(uc) * gate).astype(BF16)
    y = x + _dot(act, w_down_ref[...])
    if final_norm:
        y = _rmsnorm(y, nfin_ref[...])
    return y, tail_f


def _ffn_kernel(x_ref, xs_ref, ssf_ref, *refs, B, tT, n_tiles, Bs, final_norm, raw_out):
    w, refs = refs[:7], refs[7 + 2:]
    out_ref, of_ref, ys_ref, sof_ref, extf, *rest = refs
    i = pl.program_id(0)
    nf = (CONV_F_TAPS - 1) * B

    @pl.when(i == 0)
    def _():
        extf[0:nf, :] = jnp.zeros((nf, D_FF), F32)

    @pl.when(i < n_tiles)
    def _():
        y, tail_f = _ffn_tile(x_ref[...], w, extf, B=B, tT=tT, final_norm=final_norm)
        if raw_out:
            obuf, sem = rest
            slot = lax.rem(i, 2)
            obuf[slot] = y.reshape(tT, B, D_MODEL)

            def head_copies():
                return [pltpu.make_async_copy(obuf.at[0, N_META:tT, b, :],
                                              out_ref.at[b, 0:tT - N_META, :], sem.at[0])
                        for b in range(B)]

            def tile_copies(step, slot_):
                start = pl.multiple_of(step * tT - N_META, N_META)
                return [pltpu.make_async_copy(obuf.at[slot_, :, b, :],
                                              out_ref.at[b, pl.ds(start, tT), :], sem.at[0])
                        for b in range(B)]

            @pl.when(i == 1)
            def _():
                for c in head_copies():
                    c.wait()

            @pl.when(i > 1)
            def _():
                for c in tile_copies(i - 1, 1 - slot):
                    c.wait()

            @pl.when(i == 0)
            def _():
                for c in head_copies():
                    c.start()

            @pl.when(i > 0)
            def _():
                for c in tile_copies(i, slot):
                    c.start()

            @pl.when(i == n_tiles - 1)
            def _():
                for c in tile_copies(i, slot):
                    c.wait()
        else:
            out_ref[...] = y

        @pl.when(i == n_tiles - 1)
        def _():
            _store_state(of_ref, tail_f, CONV_F_TAPS, B)

    @pl.when(i == n_tiles)
    def _():
        _load_state(extf, ssf_ref, CONV_F_TAPS, Bs)
        y, tail_f = _ffn_tile(xs_ref[...], w, extf, B=Bs, tT=1, final_norm=final_norm)
        ys_ref[...] = y
        _store_state(sof_ref, tail_f, CONV_F_TAPS, Bs)


def _layer_spec(arr, l):
    nd = arr.ndim - 1
    return pl.BlockSpec((None,) + arr.shape[1:], lambda i: (l,) + (0,) * nd,
                        pipeline_mode=pl.Buffered(1))


def _whole_spec(arr):
    nd = arr.ndim
    return pl.BlockSpec(arr.shape, lambda i: (0,) * nd, pipeline_mode=pl.Buffered(1))


def _out_whole(shape):
    nd = len(shape)
    return pl.BlockSpec(shape, lambda i: (0,) * nd)


def _out_layer(shape, l):
    nd = len(shape)
    return pl.BlockSpec((None,) + tuple(shape), lambda i: (l,) + (0,) * nd)


_ANY = pl.BlockSpec(memory_space=pl.ANY)


_PARAMS = pltpu.CompilerParams(dimension_semantics=("arbitrary",),
                               vmem_limit_bytes=VMEM_LIMIT_BYTES_V7X)


def _mixer_call(x, meta, xs, s_states, accs, w, l, *, B, tT, rows, raw_in):
    R = B * tT
    n_tiles = rows // R
    assert n_tiles * R == rows
    Bs = xs.shape[0]
    body = functools.partial(_mixer_kernel, B=B, tT=tT, n_tiles=n_tiles, Bs=Bs, raw_in=raw_in)
    last = n_tiles - 1
    row_spec = pl.BlockSpec((R, D_MODEL), lambda i: (jnp.minimum(i, last), 0))
    if raw_in:
        assert n_tiles > 1 and tT % N_META == 0 and tT > N_META
        x_specs = [pl.BlockSpec(memory_space=pl.ANY), _whole_spec(meta)]
        x_args = [x, meta]
        x_scratch = [pltpu.VMEM((2, tT, B, D_MODEL), F32), pltpu.SemaphoreType.DMA((2,))]
    else:
        x_specs, x_args, x_scratch = [row_spec], [x], []
    ssa, ssb, ssh = s_states
    layered = [ssa, ssb, ssh, w["norm_mix"], w["w_in"], w["b_gate"], w["conv_a_w"],
               w["w_a_out"], w["conv_b_w"], w["conv_b_b"], w["w_gates"], w["rg_b_a"],
               w["rg_b_x"], w["rg_lambda"], w["w_b_out"], w["w_o"]]
    st_shapes = lambda b: [(b, CONV_A_TAPS - 1, D_CONV), (b, CONV_B_TAPS - 1, D_RNN), (b, D_RNN)]
    out_shapes = [(rows, D_MODEL)] + st_shapes(B) + [(Bs, D_MODEL)] + st_shapes(Bs)
    state_outs = (1, 2, 3, 5, 6, 7)
    n_in = len(x_args) + 1 + len(layered)
    ext_a_rows = max(R + (CONV_A_TAPS - 1) * B, CONV_A_TAPS * Bs)
    ext_b_rows = max(R + (CONV_B_TAPS - 1) * B, CONV_B_TAPS * Bs)
    rmax = max(R, Bs)
    return pl.pallas_call(
        body,
        grid=(n_tiles + 1,),
        in_specs=x_specs + [_whole_spec(xs)] + [_layer_spec(a, l) for a in layered]
                 + [_ANY] * len(accs),
        out_specs=[row_spec] + [_out_layer(s, l) if k in state_outs else _out_whole(s)
                                for k, s in enumerate(out_shapes) if k > 0],
        out_shape=[jax.ShapeDtypeStruct(((DEPTH,) + s) if k in state_outs else s, F32)
                   for k, s in enumerate(out_shapes)],
        input_output_aliases={n_in + j: k for j, k in enumerate(state_outs)},
        scratch_shapes=[pltpu.VMEM((ext_a_rows, D_CONV), F32),
                        pltpu.VMEM((ext_b_rows, D_RNN), F32),
                        pltpu.VMEM((rmax, D_RNN), F32),
                        pltpu.VMEM((max(B, Bs), D_RNN), F32),
                        pltpu.VMEM((rmax, 2 * D_RNN), F32)] + x_scratch,
        compiler_params=_PARAMS,
        name="mixer",
    )(*x_args, xs, *layered, *accs)


def _ffn_call(x, xs, ssf, accs, w, l, *, B, tT, rows, final_norm, raw_out, seq):
    R = B * tT
    n_tiles = rows // R
    assert n_tiles * R == rows
    Bs = xs.shape[0]
    body = functools.partial(_ffn_kernel, B=B, tT=tT, n_tiles=n_tiles, Bs=Bs,
                             final_norm=final_norm, raw_out=raw_out)
    last = n_tiles - 1
    row_spec = pl.BlockSpec((R, D_MODEL), lambda i: (jnp.minimum(i, last), 0))
    layered = [ssf, w["norm_ffn"], w["ffn_w_up"], w["ffn_w_gate"], w["ffn_conv_w"],
               w["ffn_conv_b"], w["ffn_w_down"]]
    if raw_out:
        assert n_tiles > 2 and tT % N_META == 0 and tT > N_META
        y_spec = pl.BlockSpec(memory_space=pl.ANY)
        y_shape = (B, seq, D_MODEL)
        y_scratch = [pltpu.VMEM((2, tT, B, D_MODEL), F32), pltpu.SemaphoreType.DMA((1,))]
    else:
        y_spec, y_shape, y_scratch = row_spec, (rows, D_MODEL), []
    out_shapes = [y_shape, (B, CONV_F_TAPS - 1, D_FF), (Bs, D_MODEL), (Bs, CONV_F_TAPS - 1, D_FF)]
    ext_f_rows = max(R + (CONV_F_TAPS - 1) * B, CONV_F_TAPS * Bs)
    return pl.pallas_call(
        body,
        grid=(n_tiles + 1,),
        in_specs=[row_spec, _whole_spec(xs)] + [_layer_spec(a, l) for a in layered]
                 + [_whole_spec(w["norm_final"])] + [_ANY] * len(accs),
        out_specs=[y_spec, _out_layer(out_shapes[1], l), _out_whole(out_shapes[2]),
                   _out_layer(out_shapes[3], l)],
        out_shape=[jax.ShapeDtypeStruct(((DEPTH,) + s) if k in (1, 3) else s, F32)
                   for k, s in enumerate(out_shapes)],
        input_output_aliases={3 + len(layered) + j: k for j, k in enumerate((1, 3))},
        scratch_shapes=[pltpu.VMEM((ext_f_rows, D_FF), F32)] + y_scratch,
        compiler_params=_PARAMS,
        name="ffn",
    )(x, xs, *layered, w["norm_final"], *accs)


def _dense_gate_weights(rg_w_a, rg_w_x):
    hpb = MXU_EDGE_V7X // RG_HEAD_DIM
    idx = jnp.arange(MXU_EDGE_V7X) // RG_HEAD_DIM
    on_diag = idx[:, None] == idx[None, :]

    def dense(wt):
        wt = wt.reshape(DEPTH, GATE_BLOCKS, hpb, RG_HEAD_DIM, RG_HEAD_DIM)
        wide = wt.transpose(0, 1, 3, 2, 4).reshape(DEPTH, GATE_BLOCKS, RG_HEAD_DIM, MXU_EDGE_V7X)
        return jnp.where(on_diag, 0.5 * jnp.tile(wide, (1, 1, hpb, 1)), 0.0)

    return jnp.concatenate([dense(rg_w_a), dense(rg_w_x)], axis=-1).astype(BF16)


def kernel(x_prompt, x_sample, state_conv_a, state_conv_b, state_rglru, state_conv_ffn,
           meta_tokens, norm_mix, norm_ffn, norm_final, w_in, b_gate, conv_a_w, w_a_out,
           conv_b_w, conv_b_b, rg_w_a, rg_b_a, rg_w_x, rg_b_x, rg_lambda, w_b_out, w_o,
           ffn_w_up, ffn_w_gate, ffn_conv_w, ffn_conv_b, ffn_w_down):
    assert PAST_LEN > 0
    vec = lambda v: v.reshape(DEPTH, 1, -1)
    w = dict(
        norm_mix=vec(norm_mix), norm_ffn=vec(norm_ffn), norm_final=norm_final.reshape(1, -1),
        w_in=w_in.astype(BF16), b_gate=vec(b_gate), conv_a_w=conv_a_w,
        w_a_out=w_a_out.astype(BF16), conv_b_w=conv_b_w, conv_b_b=vec(conv_b_b),
        w_gates=_dense_gate_weights(rg_w_a, rg_w_x), rg_b_a=vec(rg_b_a), rg_b_x=vec(rg_b_x),
        rg_lambda=vec(rg_lambda), w_b_out=w_b_out.astype(BF16), w_o=w_o.astype(BF16),
        ffn_w_up=ffn_w_up.astype(BF16), ffn_w_gate=ffn_w_gate.astype(BF16),
        ffn_conv_w=ffn_conv_w, ffn_conv_b=vec(ffn_conv_b), ffn_w_down=ffn_w_down.astype(BF16))

    bp, seq, _ = x_prompt.shape
    dt = x_prompt.dtype
    bs, t_s, _ = x_sample.shape
    assert t_s == 1
    rows = (N_META + seq) * bp
    x = x_prompt
    xs = x_sample.reshape(bs, D_MODEL)
    meta = meta_tokens.astype(dt)
    acc = lambda b, *tail: jnp.zeros((DEPTH, b) + tail, dt)
    m_accs = [acc(bp, CONV_A_TAPS - 1, D_CONV), acc(bp, CONV_B_TAPS - 1, D_RNN), acc(bp, D_RNN),
              acc(bs, CONV_A_TAPS - 1, D_CONV), acc(bs, CONV_B_TAPS - 1, D_RNN), acc(bs, D_RNN)]
    f_accs = [acc(bp, CONV_F_TAPS - 1, D_FF), acc(bs, CONV_F_TAPS - 1, D_FF)]
    for l in range(DEPTH):
        x, pa, pb, ph, xs, sa, sb, sh = _mixer_call(
            x, meta, xs, (state_conv_a, state_conv_b, state_rglru), m_accs, w, l,
            B=bp, tT=PROMPT_TILE_STEPS, rows=rows, raw_in=(l == 0))
        m_accs = [pa, pb, ph, sa, sb, sh]
        last = l == DEPTH - 1
        x, pf, xs, sf = _ffn_call(x, xs, state_conv_ffn, f_accs, w, l, B=bp,
                                  tT=PROMPT_TILE_STEPS if last else FFN_TILE_STEPS,
                                  rows=rows, final_norm=last, raw_out=last, seq=seq)
        f_accs = [pf, sf]
    y_sample = xs.reshape(bs, t_s, D_MODEL)
    p_a, p_b, p_h, s_a, s_b, s_h = m_accs
    p_f, s_f = f_accs
    return (x, y_sample, p_a, p_b, p_h, p_f, s_a, s_b, s_h, s_f)
```
